```python
import math
import jax, jax.numpy as jnp
from jax import lax
import numpy as np

D_MODEL = 2048
BATCH = 1
SEQ = 8192
DEPTH = 1
DEC_BATCH = 32
DEC_SEQ = 8
PAST_LEN = 8192
PAGE_SIZE = 128

MIX_W = D_MODEL
M_W = MIX_W // 2
M_H = 4
M_D = M_W // M_H
A_W = MIX_W - M_W
A_DV = 128
A_H = A_W // A_DV
A_DK = A_DV // 2
MLSTM_CHUNK = 64
Q_BLOCK = 128
ROPE_THETA = 10000.0
EPS = 1e-6
F_BIAS_LO = 3.0
F_BIAS_HI = 6.0
IN_COLS = 5 * M_W + 2 * M_H + 4 * A_W

kernel_name = 'hymba_mlstm_diffattn_step'


def _split_points():
    sizes = [M_W] * 5 + [M_H, M_H] + [A_W] * 4
    return [int(s) for s in np.cumsum(sizes)[:-1]]


def rmsnorm(x, g):
    xf = x.astype(jnp.float32)
    return xf * lax.rsqrt(jnp.mean(xf * xf, axis=-1, keepdims=True) + EPS) * g.astype(jnp.float32)


def rope(x, pos):
    half = A_DK // 2
    inv = ROPE_THETA ** (-jnp.arange(half, dtype=jnp.float32) * 2.0 / A_DK)
    ang = pos.astype(jnp.float32)[:, None] * inv[None, :]
    cos = jnp.cos(ang)[None, :, None, None, :]
    sin = jnp.sin(ang)[None, :, None, None, :]
    xf = x.astype(jnp.float32)
    x1, x2 = xf[..., :half], xf[..., half:]
    return jnp.concatenate([x1 * cos - x2 * sin, x2 * cos + x1 * sin], axis=-1)


def mlstm_chunkwise(q, k, v, ig, lf, C0, n0, m0):
    B, T, H, D = q.shape
    L = math.gcd(T, MLSTM_CHUNK)
    NC = T // L
    f32 = jnp.float32

    def to_chunks(a):
        return a.astype(f32).reshape((B, NC, L) + a.shape[2:]).swapaxes(0, 1)

    causal = jnp.tril(jnp.ones((L, L), dtype=bool))

    def step(carry, xs):
        C, n, m = carry
        qc, kc, vc, ic, fc = xs
        bh = jnp.cumsum(fc, axis=1).transpose(0, 2, 1)
        ih = ic.transpose(0, 2, 1)
        logd = bh[..., :, None] - bh[..., None, :] + ih[..., None, :]
        logd = jnp.where(causal, logd, -jnp.inf)
        m_t = jnp.maximum(bh + m[..., None], jnp.max(logd, axis=-1))
        dmat = jnp.exp(logd - m_t[..., None])
        inter = jnp.exp(bh + m[..., None] - m_t)
        w = jnp.einsum('blhd,bshd->bhls', qc, kc) * dmat
        num = jnp.einsum('bhls,bshv->bhlv', w, vc) + inter[..., None] * jnp.einsum('bhvd,blhd->bhlv', C, qc)
        den = jnp.sum(w, axis=-1) + inter * jnp.einsum('bhd,blhd->bhl', n, qc)
        h = num / jnp.maximum(jnp.abs(den), jnp.exp(-m_t))[..., None]
        m_new = m_t[..., -1]
        g = jnp.exp(bh[..., -1:] - bh + ih - m_new[..., None])
        decay = jnp.exp(bh[..., -1] + m - m_new)
        C_new = decay[..., None, None] * C + jnp.einsum('bhs,bshv,bshd->bhvd', g, vc, kc)
        n_new = decay[..., None] * n + jnp.einsum('bhs,bshd->bhd', g, kc)
        return (C_new, n_new, m_new), h.transpose(0, 2, 1, 3)

    carry0 = (C0.astype(f32), n0.astype(f32), m0.astype(f32))
    (C, n, m), hs = lax.scan(step, carry0, (to_chunks(q), to_chunks(k), to_chunks(v), to_chunks(ig), to_chunks(lf)))
    return hs.swapaxes(0, 1).reshape(B, T, H, D), C, n, m


def diff_scores(q, k):
    return jnp.einsum('bqhmd,bkhmd->bhmqk', q.astype(jnp.float32), k.astype(jnp.float32)) * (A_DK ** -0.5)


def diff_combine(s, mask, lam):
    p = jax.nn.softmax(jnp.where(mask, s, -jnp.inf), axis=-1)
    return p[:, :, 0] - lam * p[:, :, 1]


def mixer_sublayer(x, pos, l, C0, n0, m0, attend, w_in, b_i, b_f, w_out, g_pre, g_post,
                   g_mlstm, g_diff, lam_q1, lam_k1, lam_q2, lam_k2):
    B, T, _ = x.shape
    h = rmsnorm(x, g_pre[l]).astype(x.dtype)
    proj = jnp.einsum('btd,dc->btc', h, w_in[l])
    mq, mk, mv, mo, mz, ig, fg, aq, ak, av, az = jnp.split(proj, _split_points(), axis=-1)
    mh = lambda a: a.reshape(B, T, M_H, M_D)
    igate = (ig + b_i[l]).astype(jnp.float32)
    lf = jax.nn.log_sigmoid((fg + b_f[l]).astype(jnp.float32))
    hm, C, n, m = mlstm_chunkwise(mh(mq), mh(mk) * (M_D ** -0.5), mh(mv), igate, lf, C0, n0, m0)
    hm = jax.nn.sigmoid(mh(mo).astype(jnp.float32)) * hm
    hm = rmsnorm(hm, g_mlstm[l].reshape(M_H, M_D)).reshape(B, T, M_W)
    q = rope(aq.reshape(B, T, A_H, 2, A_DK), pos)
    k = rope(ak.reshape(B, T, A_H, 2, A_DK), pos)
    v = av.reshape(B, T, A_H, A_DV)
    lam_init = 0.8 - 0.6 * math.exp(-0.3 * l)
    lam = (jnp.exp(jnp.sum(lam_q1[l].astype(jnp.float32) * lam_k1[l].astype(jnp.float32)))
           - jnp.exp(jnp.sum(lam_q2[l].astype(jnp.float32) * lam_k2[l].astype(jnp.float32))) + lam_init)
    ha = attend(q, k, v, lam)
    ha = (rmsnorm(ha, g_diff[l].reshape(A_H, A_DV)) * (1.0 - lam_init)).reshape(B, T, A_W)
    mixed = jnp.concatenate([hm * jax.nn.silu(mz.astype(jnp.float32)),
                             ha * jax.nn.silu(az.astype(jnp.float32))], axis=-1).astype(x.dtype)
    y = jnp.einsum('btc,cd->btd', mixed, w_out[l])
    x = x + rmsnorm(y, g_post[l]).astype(x.dtype)
    return x, k.reshape(B, T, A_H, 2 * A_DK), v, C, n, m


def setup_inputs(seed: int = 0) -> dict:
    key = jax.random.key(seed)
    ks = jax.random.split(key, 24)
    n_pages = PAST_LEN // PAGE_SIZE
    used = DEC_BATCH * n_pages
    pool = (used * 5 + 3) // 4
    nrm = jax.random.normal
    page_table = jax.random.permutation(ks[0], pool)[:used].reshape(DEC_BATCH, n_pages).astype(jnp.int32)
    f_bias = jnp.linspace(F_BIAS_LO, F_BIAS_HI, M_H, dtype=jnp.float32)[None, :]
    return {
        'x_prompt': nrm(ks[1], (BATCH, SEQ, D_MODEL), jnp.float32),
        'x_sample': nrm(ks[2], (DEC_BATCH, DEC_SEQ, D_MODEL), jnp.float32),
        'cache_k': nrm(ks[3], (DEPTH, pool, PAGE_SIZE, A_H, 2 * A_DK), jnp.float32),
        'cache_v': nrm(ks[4], (DEPTH, pool, PAGE_SIZE, A_H, A_DV), jnp.float32),
        'state_C': 0.1 * nrm(ks[5], (DEPTH, DEC_BATCH, M_H, M_D, M_D), jnp.float32),
        'state_n': nrm(ks[6], (DEPTH, DEC_BATCH, M_H, M_D), jnp.float32),
        'state_m': jax.random.uniform(ks[7], (DEPTH, DEC_BATCH, M_H), jnp.float32, 0.0, 2.0),
        'page_table': page_table,
        'w_in': nrm(ks[8], (DEPTH, D_MODEL, IN_COLS), jnp.float32) * D_MODEL ** -0.5,
        'b_i': 0.1 * nrm(ks[9], (DEPTH, M_H), jnp.float32),
        'b_f': f_bias + 0.1 * nrm(ks[10], (DEPTH, M_H), jnp.float32),
        'w_out': nrm(ks[11], (DEPTH, MIX_W, D_MODEL), jnp.float32) * MIX_W ** -0.5,
        'g_pre': 1.0 + 0.05 * nrm(ks[12], (DEPTH, D_MODEL), jnp.float32),
        'g_post': 1.0 + 0.05 * nrm(ks[13], (DEPTH, D_MODEL), jnp.float32),
        'g_mlstm': 1.0 + 0.05 * nrm(ks[14], (DEPTH, M_W), jnp.float32),
        'g_diff': 1.0 + 0.05 * nrm(ks[15], (DEPTH, A_W), jnp.float32),
        'lam_q1': 0.1 * nrm(ks[16], (DEPTH, A_DK), jnp.float32),
        'lam_k1': 0.1 * nrm(ks[17], (DEPTH, A_DK), jnp.float32),
        'lam_q2': 0.1 * nrm(ks[18], (DEPTH, A_DK), jnp.float32),
        'lam_k2': 0.1 * nrm(ks[19], (DEPTH, A_DK), jnp.float32),
    }


def reference(x_prompt, x_sample, cache_k, cache_v, state_C, state_n, state_m, page_table,
              w_in, b_i, b_f, w_out, g_pre, g_post, g_mlstm, g_diff, lam_q1, lam_k1, lam_q2, lam_k2):
    B, T, _ = x_prompt.shape
    DB, TS, _ = x_sample.shape
    n_past = page_table.shape[1] * PAGE_SIZE
    pos_p = jnp.arange(T)
    pos_s = n_past + jnp.arange(TS)
    qb = math.gcd(T, Q_BLOCK)
    nqb = T // qb
    mask_s = jnp.concatenate([jnp.ones((TS, n_past), dtype=bool), jnp.tril(jnp.ones((TS, TS), dtype=bool))], axis=1)
    weights = (w_in, b_i, b_f, w_out, g_pre, g_post, g_mlstm, g_diff, lam_q1, lam_k1, lam_q2, lam_k2)

    def prompt_attend(q, k, v, lam):
        q_blocks = q.reshape(B, nqb, qb, A_H, 2, A_DK).swapaxes(0, 1)
        p_blocks = pos_p.reshape(nqb, qb)
        v32 = v.astype(jnp.float32)

        def block(args):
            qblk, pblk = args
            a = diff_combine(diff_scores(qblk, k), pos_p[None, :] <= pblk[:, None], lam)
            return jnp.einsum('bhqk,bkhv->bqhv', a, v32)

        out = lax.map(block, (q_blocks, p_blocks))
        return out.swapaxes(0, 1).reshape(B, T, A_H, A_DV)

    xp, xs = x_prompt, x_sample
    kp_l, vp_l, Cp_l, np_l, mp_l = [], [], [], [], []
    ks_l, vs_l, Cs_l, ns_l, ms_l = [], [], [], [], []
    for l in range(DEPTH):
        zC = jnp.zeros((B, M_H, M_D, M_D), jnp.float32)
        zn = jnp.zeros((B, M_H, M_D), jnp.float32)
        zm = jnp.zeros((B, M_H), jnp.float32)
        xp, kk, vv, C, n, m = mixer_sublayer(xp, pos_p, l, zC, zn, zm, prompt_attend, *weights)
        kp_l.append(kk); vp_l.append(vv); Cp_l.append(C); np_l.append(n); mp_l.append(m)

        k_past = cache_k[l][page_table].reshape(DB, n_past, A_H, 2, A_DK)
        v_past = cache_v[l][page_table].reshape(DB, n_past, A_H, A_DV)

        def sample_attend(q, k, v, lam, k_past=k_past, v_past=v_past):
            s = jnp.concatenate([diff_scores(q, k_past), diff_scores(q, k)], axis=-1)
            a = diff_combine(s, mask_s, lam)
            return (jnp.einsum('bhqk,bkhv->bqhv', a[..., :n_past], v_past.astype(jnp.float32))
                    + jnp.einsum('bhqk,bkhv->bqhv', a[..., n_past:], v.astype(jnp.float32)))

        xs, kk, vv, C, n, m = mixer_sublayer(xs, pos_s, l, state_C[l], state_n[l], state_m[l], sample_attend, *weights)
        ks_l.append(kk); vs_l.append(vv); Cs_l.append(C); ns_l.append(n); ms_l.append(m)

    return (xp, xs,
            jnp.stack(kp_l), jnp.stack(vp_l), jnp.stack(Cp_l), jnp.stack(np_l), jnp.stack(mp_l),
            jnp.stack(ks_l), jnp.stack(vs_l), jnp.stack(Cs_l), jnp.stack(ns_l), jnp.stack(ms_l))
```

```python
import functools
import math

import jax
import jax.numpy as jnp
from jax import lax
from jax.experimental import pallas as pl
from jax.experimental.pallas import tpu as pltpu

F32 = jnp.float32
BF16 = jnp.bfloat16

D_MODEL = 2048
M_H = 4
M_D = 256
M_W = M_H * M_D
A_H = 8
A_DV = 128
A_DK = 64
A_W = A_H * A_DV
PAGE = 128
ROPE_THETA = 10000.0
EPS = 1e-6
PIECE = 1024
GATE_PAD = 128
V7X_VMEM_LIMIT = 56 * 1024 * 1024

_P_MQ, _P_MK, _P_MV, _P_AQ, _P_AK, _P_AV, _P_MO, _P_MZ, _P_AZ = range(9)
N_PIECES = 9
N_SLAB = 6
N_FSLAB = 3


def _dot(a, b):
    return jnp.dot(a, b, preferred_element_type=F32)


def _dot_nt(a, b):
    return lax.dot_general(a, b, (((1,), (1,)), ((), ())), preferred_element_type=F32)


def _dot_tn(a, b):
    return lax.dot_general(a, b, (((0,), (0,)), ((), ())), preferred_element_type=F32)


def _split2(a):
    hi = a.astype(BF16)
    lo = (a - hi.astype(F32)).astype(BF16)
    return hi, lo


def _split3(a):
    a1 = a.astype(BF16)
    r1 = a - a1.astype(F32)
    a2 = r1.astype(BF16)
    a3 = (r1 - a2.astype(F32)).astype(BF16)
    return a1, a2, a3


def _sigmoid(x):
    return 1.0 / (1.0 + jnp.exp(-x))


def _log_sigmoid(x):
    return jnp.minimum(x, 0.0) - jnp.log(1.0 + jnp.exp(-jnp.abs(x)))


def _inproj_kernel(x_ref, gpre_ref, w_ref, wg_ref, wgt_ref, cos_ref, sin_ref,
                   slab_ref, k_ref, v_ref, fslab_ref, gates_ref, gatest_ref, h_ref):
    j = pl.program_id(1)
    tm = x_ref.shape[0]

    @pl.when(j == 0)
    def _():
        x = x_ref[...]
        h = x * lax.rsqrt(jnp.mean(x * x, axis=-1, keepdims=True) + EPS) * gpre_ref[...]
        h_hi, h_lo = _split2(h)
        h_ref[...] = h_hi
        wg_hi, wg_lo = _split2(wg_ref[...])
        gates_ref[...] = _dot(h_lo, wg_hi) + _dot(h_hi, wg_lo) + _dot(h_hi, wg_hi)
        wgt_hi, wgt_lo = _split2(wgt_ref[...])
        gatest_ref[...] = _dot_nt(wgt_hi, h_lo) + _dot_nt(wgt_lo, h_hi) + _dot_nt(wgt_hi, h_hi)

    acc = _dot(h_ref[...], w_ref[...])

    def rope_chunks(a):
        cos = cos_ref[...]
        sin = sin_ref[...]
        lane = lax.broadcasted_iota(jnp.int32, (tm, A_DV), 1)
        first_half = (lane % A_DK) < (A_DK // 2)
        for c in range(PIECE // A_DV):
            xc = a[:, c * A_DV:(c + 1) * A_DV]
            rot = jnp.where(first_half, pltpu.roll(xc, A_DV - A_DK // 2, 1), pltpu.roll(xc, A_DK // 2, 1))
            yield c, xc * cos + rot * sin

    @pl.when((j == _P_MQ) | (j == _P_MV))
    def _():
        slab_ref[...] = acc.astype(slab_ref.dtype)

    @pl.when(j == _P_MK)
    def _():
        slab_ref[...] = (acc * (M_D ** -0.5)).astype(slab_ref.dtype)

    @pl.when(j == _P_AQ)
    def _():
        for c, y in rope_chunks(acc):
            slab_ref[:, c * A_DV:(c + 1) * A_DV] = (y * (A_DK ** -0.5)).astype(slab_ref.dtype)

    @pl.when(j == _P_AK)
    def _():
        for c, y in rope_chunks(acc):
            k_ref[:, c * A_DV:(c + 1) * A_DV] = y
            slab_ref[:, c * A_DV:(c + 1) * A_DV] = y.astype(slab_ref.dtype)

    @pl.when(j == _P_AV)
    def _():
        v_ref[...] = acc
        slab_ref[...] = acc.astype(slab_ref.dtype)

    @pl.when(j >= N_SLAB)
    def _():
        fslab_ref[...] = acc


def _inproj(x, g_pre, w_main, wg, wgt, cos, sin, *, tm, slab_dtype):
    rows = x.shape[0]
    assert rows % tm == 0
    grid = (rows // tm, N_PIECES)
    row_blk = lambda width: pl.BlockSpec((tm, width), lambda i, j: (i, 0))
    return pl.pallas_call(
        _inproj_kernel,
        grid=grid,
        in_specs=[
            row_blk(D_MODEL),
            pl.BlockSpec((1, D_MODEL), lambda i, j: (0, 0)),
            pl.BlockSpec((D_MODEL, PIECE), lambda i, j: (0, j)),
            pl.BlockSpec((D_MODEL, GATE_PAD), lambda i, j: (0, 0)),
            pl.BlockSpec((8, D_MODEL), lambda i, j: (0, 0)),
            row_blk(A_DV),
            row_blk(A_DV),
        ],
        out_specs=[
            pl.BlockSpec((tm, PIECE), lambda i, j: (i, jnp.minimum(j, N_SLAB - 1))),
            row_blk(PIECE),
            row_blk(PIECE),
            pl.BlockSpec((tm, PIECE), lambda i, j: (i, jnp.maximum(j - N_SLAB, 0))),
            row_blk(GATE_PAD),
            pl.BlockSpec((8, tm), lambda i, j: (0, i)),
        ],
        out_shape=[
            jax.ShapeDtypeStruct((rows, N_SLAB * PIECE), slab_dtype),
            jax.ShapeDtypeStruct((rows, PIECE), F32),
            jax.ShapeDtypeStruct((rows, PIECE), F32),
            jax.ShapeDtypeStruct((rows, N_FSLAB * PIECE), F32),
            jax.ShapeDtypeStruct((rows, GATE_PAD), F32),
            jax.ShapeDtypeStruct((8, rows), F32),
        ],
        scratch_shapes=[pltpu.VMEM((tm, D_MODEL), BF16)],
        compiler_params=pltpu.CompilerParams(
            dimension_semantics=("arbitrary", "arbitrary"), vmem_limit_bytes=V7X_VMEM_LIMIT),
        name="inproj",
    )(x, g_pre, w_main, wg, wgt, cos, sin)


def _mlstm_kernel(*refs, L, has_state):
    if has_state:
        (q_ref, k_ref, v_ref, mo_ref, mz_ref, g_ref, gt_ref, bias_ref, biast_ref, gm_ref,
         c0_ref, n0_ref, m0_ref, out_ref, cout_ref, nout_ref, mout_ref, c_s, n_s, m_s) = refs
    else:
        (q_ref, k_ref, v_ref, mo_ref, mz_ref, g_ref, gt_ref, bias_ref, biast_ref, gm_ref,
         out_ref, cout_ref, nout_ref, mout_ref, c_s, n_s, m_s) = refs
    hd = pl.program_id(1)
    c = pl.program_id(2)

    @pl.when(c == 0)
    def _():
        if has_state:
            c_s[...] = c0_ref[...]
            n_s[...] = n0_ref[...]
            m_s[...] = m0_ref[...]
        else:
            c_s[...] = jnp.zeros_like(c_s)
            n_s[...] = jnp.zeros_like(n_s)
            m_s[...] = jnp.zeros_like(m_s)

    q = q_ref[...].astype(BF16)
    k = k_ref[...].astype(BF16)
    v = v_ref[...].astype(BF16)

    gcol = g_ref[...] + bias_ref[...]
    grow = gt_ref[...] + biast_ref[...]
    lane = lax.broadcasted_iota(jnp.int32, (L, GATE_PAD), 1)
    sub = lax.broadcasted_iota(jnp.int32, (8, L), 0)
    r_i = lax.broadcasted_iota(jnp.int32, (L, L), 0)
    c_i = lax.broadcasted_iota(jnp.int32, (L, L), 1)
    causal = r_i >= c_i
    lf_col = _log_sigmoid(gcol)
    lf_row = _log_sigmoid(grow)
    if L >= 128:
        tri = jnp.where(causal, 1.0, 0.0).astype(BF16)
        trit = jnp.where(r_i <= c_i, 1.0, 0.0).astype(BF16)
        a1, a2, a3 = _split3(lf_col)
        bcols = _dot(tri, a3) + _dot(tri, a2) + _dot(tri, a1)
        a1, a2, a3 = _split3(lf_row)
        brows = _dot(a3, trit) + _dot(a2, trit) + _dot(a1, trit)
    else:
        rows = [lf_col[0:1, :]]
        for t in range(1, L):
            rows.append(rows[-1] + lf_col[t:t + 1, :])
        bcols = jnp.concatenate(rows, axis=0)
        lane_t = lax.broadcasted_iota(jnp.int32, (8, L), 1)
        brows = jnp.zeros((8, L), F32)
        for s in range(L):
            brows = brows + jnp.where(lane_t >= s, lf_row[:, s:s + 1], 0.0)
    i_col = jnp.sum(jnp.where(lane == hd, gcol, 0.0), axis=1, keepdims=True)
    b_col = jnp.sum(jnp.where(lane == hd + M_H, bcols, 0.0), axis=1, keepdims=True)
    i_row = jnp.sum(jnp.where(sub == hd, grow, 0.0), axis=0, keepdims=True)
    b_row = jnp.sum(jnp.where(sub == hd + M_H, brows, 0.0), axis=0, keepdims=True)

    m_prev = m_s[...]
    logd = jnp.where(causal, b_col - b_row + i_row, -jnp.inf)
    m_col = jnp.maximum(b_col + m_prev, jnp.max(logd, axis=1, keepdims=True))
    dmat = jnp.exp(logd - m_col)
    inter = jnp.exp(b_col + m_prev - m_col)

    cmat = c_s[...]
    nvec = n_s[...]
    w = _dot_nt(q, k) * dmat
    num = _dot(w.astype(BF16), v) + inter * _dot_nt(q, cmat.astype(BF16))
    qf = q.astype(F32)
    den = jnp.sum(w, axis=1, keepdims=True) + inter * jnp.sum(qf * nvec, axis=1, keepdims=True)
    hstate = num / jnp.maximum(jnp.abs(den), jnp.exp(-m_col))

    m_new = m_col[L - 1:L, :]
    b_last = b_col[L - 1:L, :]
    gdec = jnp.exp(b_last - b_col + i_col - m_new)
    decay = jnp.exp(b_last + m_prev - m_new)
    gk = gdec * k.astype(F32)
    c_new = decay * cmat + _dot_tn(v, gk.astype(BF16))
    n_new = decay * nvec + jnp.sum(gk, axis=0, keepdims=True)
    c_s[...] = c_new
    n_s[...] = n_new
    m_s[...] = m_new

    hm = _sigmoid(mo_ref[...]) * hstate
    hm = hm * lax.rsqrt(jnp.mean(hm * hm, axis=-1, keepdims=True) + EPS) * gm_ref[...]
    mz = mz_ref[...]
    out_ref[...] = (hm * (mz * _sigmoid(mz))).astype(out_ref.dtype)

    @pl.when(c == pl.num_programs(2) - 1)
    def _():
        cout_ref[...] = c_new
        nout_ref[...] = n_new
        mout_ref[...] = m_new


def _mlstm(slab, fslab, gates, gatest3, bias_row, bias_col, g_mlstm, state, *, batch, seq, L, out_dtype):
    nc = seq // L
    assert nc * L == seq
    has_state = state is not None
    tok = lambda off: pl.BlockSpec((L, M_D), lambda b, h, c: (b * nc + c, off + h))
    const = lambda shape: pl.BlockSpec(shape, lambda b, h, c: (0,) * len(shape))
    per_bh = lambda r, cdim: pl.BlockSpec((None, None, r, cdim), lambda b, h, c: (b, h, 0, 0))
    in_specs = [
        tok(0 * M_H), tok(1 * M_H), tok(2 * M_H),
        tok(0 * M_H), tok(1 * M_H),
        pl.BlockSpec((L, GATE_PAD), lambda b, h, c: (b * nc + c, 0)),
        pl.BlockSpec((None, 8, L), lambda b, h, c: (b * nc + c, 0, 0)),
        const((1, GATE_PAD)), const((8, 1)),
        pl.BlockSpec((1, M_D), lambda b, h, c: (0, h)),
    ]
    args = [slab, slab, slab, fslab, fslab, gates, gatest3, bias_row, bias_col, g_mlstm]
    if has_state:
        in_specs += [per_bh(M_D, M_D), per_bh(1, M_D), per_bh(1, 1)]
        args += list(state)
    return pl.pallas_call(
        functools.partial(_mlstm_kernel, L=L, has_state=has_state),
        grid=(batch, M_H, nc),
        in_specs=in_specs,
        out_specs=[
            pl.BlockSpec((L, M_D), lambda b, h, c: (b * nc + c, h)),
            per_bh(M_D, M_D), per_bh(1, M_D), per_bh(1, 1),
        ],
        out_shape=[
            jax.ShapeDtypeStruct((batch * seq, M_W), out_dtype),
            jax.ShapeDtypeStruct((batch, M_H, M_D, M_D), F32),
            jax.ShapeDtypeStruct((batch, M_H, 1, M_D), F32),
            jax.ShapeDtypeStruct((batch, M_H, 1, 1), F32),
        ],
        scratch_shapes=[pltpu.VMEM((M_D, M_D), F32), pltpu.VMEM((1, M_D), F32), pltpu.VMEM((1, 1), F32)],
        compiler_params=pltpu.CompilerParams(
            dimension_semantics=("arbitrary", "arbitrary", "arbitrary"), vmem_limit_bytes=V7X_VMEM_LIMIT),
        name="mlstm_state" if has_state else "mlstm_fresh",
    )(*args)


def _lambda(lam_ref, lam_init):
    lam = lam_ref[...]
    s1 = jnp.sum(lam[0:1, :] * lam[1:2, :], axis=1, keepdims=True)
    s2 = jnp.sum(lam[2:3, :] * lam[3:4, :], axis=1, keepdims=True)
    return jnp.exp(s1) - jnp.exp(s2) + lam_init


def _stack_maps(qh):
    lane = lax.broadcasted_iota(jnp.int32, qh.shape, 1)
    zero = jnp.zeros_like(qh)
    return jnp.concatenate([jnp.where(lane < A_DK, qh, zero), jnp.where(lane >= A_DK, qh, zero)], axis=0)


def _diff_finish(acc, l, n, lam, gd, az, lam_init):
    o = acc / l
    ha = o[:n, :] - lam * o[n:, :]
    ha = ha * lax.rsqrt(jnp.mean(ha * ha, axis=-1, keepdims=True) + EPS) * gd * (1.0 - lam_init)
    return ha * (az * _sigmoid(az))


def _prompt_attn_kernel(q_ref, k_ref, v_ref, az_ref, gd_ref, lam_ref, out_ref, m_s, l_s, acc_s, *, tq, lam_init):
    qi = pl.program_id(1)
    qs = _stack_maps(q_ref[...])
    m_s[...] = jnp.full_like(m_s, -jnp.inf)
    l_s[...] = jnp.zeros_like(l_s)
    acc_s[...] = jnp.zeros_like(acc_s)

    def tile(ki, masked):
        start = pl.multiple_of(ki * tq, tq)
        kt = k_ref[pl.ds(start, tq), :]
        vt = v_ref[pl.ds(start, tq), :]
        s = _dot_nt(qs, kt)
        if masked:
            r = lax.broadcasted_iota(jnp.int32, (2 * tq, tq), 0)
            cc = lax.broadcasted_iota(jnp.int32, (2 * tq, tq), 1)
            r = jnp.where(r >= tq, r - tq, r)
            s = jnp.where(r >= cc, s, -jnp.inf)
        m_old = m_s[...]
        m_new = jnp.maximum(m_old, jnp.max(s, axis=1, keepdims=True))
        alpha = jnp.exp(m_old - m_new)
        p = jnp.exp(s - m_new)
        l_s[...] = alpha * l_s[...] + jnp.sum(p, axis=1, keepdims=True)
        acc_s[...] = alpha * acc_s[...] + _dot(p.astype(BF16), vt)
        m_s[...] = m_new

    def body(ki, carry):
        tile(ki, False)
        return carry

    lax.fori_loop(0, qi, body, 0)
    tile(qi, True)

    lam = _lambda(lam_ref, lam_init)
    out_ref[...] = _diff_finish(acc_s[...], l_s[...], tq, lam, gd_ref[...], az_ref[...], lam_init).astype(out_ref.dtype)


def _prompt_attn(slab, fslab, g_diff, lam_params, *, seq, tq, lam_init):
    nq = seq // tq
    assert nq * tq == seq
    blk = PIECE // A_DV
    return pl.pallas_call(
        functools.partial(_prompt_attn_kernel, tq=tq, lam_init=lam_init),
        grid=(A_H, nq),
        in_specs=[
            pl.BlockSpec((tq, A_DV), lambda h, i: (i, _P_AQ * blk + h)),
            pl.BlockSpec((seq, A_DV), lambda h, i: (0, _P_AK * blk + h)),
            pl.BlockSpec((seq, A_DV), lambda h, i: (0, _P_AV * blk + h)),
            pl.BlockSpec((tq, A_DV), lambda h, i: (i, (_P_AZ - N_SLAB) * blk + h)),
            pl.BlockSpec((1, A_DV), lambda h, i: (0, h)),
            pl.BlockSpec((4, A_DK), lambda h, i: (0, 0)),
        ],
        out_specs=pl.BlockSpec((tq, A_DV), lambda h, i: (i, h)),
        out_shape=jax.ShapeDtypeStruct((seq, A_W), BF16),
        scratch_shapes=[pltpu.VMEM((2 * tq, 1), F32), pltpu.VMEM((2 * tq, 1), F32), pltpu.VMEM((2 * tq, A_DV), F32)],
        compiler_params=pltpu.CompilerParams(
            dimension_semantics=("arbitrary", "arbitrary"), vmem_limit_bytes=V7X_VMEM_LIMIT),
        name="prompt_attn",
    )(slab, slab, slab, fslab, g_diff, lam_params)


def _sample_attn_kernel(pt_ref, q_ref, kn_ref, vn_ref, az_ref, ck_ref, cv_ref, gd_ref, lam_ref, out_ref,
                        qs_s, m_s, l_s, acc_s, *, ts, lam_init):
    del pt_ref
    p = pl.program_id(1)
    n2 = 2 * ts

    @pl.when(p == 0)
    def _():
        q = q_ref[...]
        for h in range(A_H):
            qs_s[h] = _stack_maps(q[:, h * A_DV:(h + 1) * A_DV]).astype(BF16)
        m_s[...] = jnp.full_like(m_s, -jnp.inf)
        l_s[...] = jnp.zeros_like(l_s)
        acc_s[...] = jnp.zeros_like(acc_s)

    def update(h, s, vt):
        m_old = m_s[h]
        m_new = jnp.maximum(m_old, jnp.max(s, axis=1, keepdims=True))
        alpha = jnp.exp(m_old - m_new)
        pr = jnp.exp(s - m_new)
        l_s[h] = alpha * l_s[h] + jnp.sum(pr, axis=1, keepdims=True)
        acc_s[h] = alpha * acc_s[h] + _dot(pr.astype(BF16), vt)
        m_s[h] = m_new

    for h in range(A_H):
        kt = ck_ref[pl.ds(h, PAGE, stride=A_H), :].astype(BF16)
        vt = cv_ref[pl.ds(h, PAGE, stride=A_H), :].astype(BF16)
        update(h, _dot_nt(qs_s[h], kt), vt)

    @pl.when(p == pl.num_programs(1) - 1)
    def _():
        lam = _lambda(lam_ref, lam_init)
        kn = kn_ref[...]
        vn = vn_ref[...]
        az = az_ref[...]
        gd = gd_ref[...]
        r = lax.broadcasted_iota(jnp.int32, (n2, PAGE), 0)
        cc = lax.broadcasted_iota(jnp.int32, (n2, PAGE), 1)
        r = jnp.where(r >= ts, r - ts, r)
        pad = jnp.zeros((PAGE - ts, A_DV), F32)
        for h in range(A_H):
            cols = slice(h * A_DV, (h + 1) * A_DV)
            kt = jnp.concatenate([kn[:, cols], pad], axis=0).astype(BF16)
            vt = jnp.concatenate([vn[:, cols], pad], axis=0).astype(BF16)
            s = jnp.where(r >= cc, _dot_nt(qs_s[h], kt), -jnp.inf)
            update(h, s, vt)
            out_ref[:, cols] = _diff_finish(acc_s[h], l_s[h], ts, lam, gd[:, cols], az[:, cols], lam_init)


def _sample_attn(page_table, slab, k_new, v_new, fslab, cache_k, cache_v, g_diff, lam_params, *, batch, ts, lam_init):
    n_pages = page_table.shape[1]
    pool = cache_k.shape[0]
    ck = cache_k.reshape(pool, PAGE * A_H, A_DV)
    cv = cache_v.reshape(pool, PAGE * A_H, A_DV)
    n2 = 2 * ts
    tok = lambda col: pl.BlockSpec((ts, PIECE), lambda b, p, pt: (b, col))
    page = pl.BlockSpec((None, PAGE * A_H, A_DV), lambda b, p, pt: (pt[b, p], 0, 0))
    grid_spec = pltpu.PrefetchScalarGridSpec(
        num_scalar_prefetch=1,
        grid=(batch, n_pages),
        in_specs=[
            tok(_P_AQ), tok(0), tok(0), tok(_P_AZ - N_SLAB), page, page,
            pl.BlockSpec((1, A_W), lambda b, p, pt: (0, 0)),
            pl.BlockSpec((4, A_DK), lambda b, p, pt: (0, 0)),
        ],
        out_specs=pl.BlockSpec((ts, A_W), lambda b, p, pt: (b, 0)),
        scratch_shapes=[
            pltpu.VMEM((A_H, n2, A_DV), BF16),
            pltpu.VMEM((A_H, n2, 1), F32), pltpu.VMEM((A_H, n2, 1), F32), pltpu.VMEM((A_H, n2, A_DV), F32),
        ],
    )
    return pl.pallas_call(
        functools.partial(_sample_attn_kernel, ts=ts, lam_init=lam_init),
        grid_spec=grid_spec,
        out_shape=jax.ShapeDtypeStruct((batch * ts, A_W), F32),
        compiler_params=pltpu.CompilerParams(
            dimension_semantics=("arbitrary", "arbitrary"), vmem_limit_bytes=V7X_VMEM_LIMIT),
        name="sample_attn",
    )(page_table, slab, k_new, v_new, fslab, ck, cv, g_diff, lam_params)


def _outproj_kernel(mm_ref, ma_ref, x_ref, w_ref, g_ref, out_ref):
    y = (_dot(mm_ref[...].astype(BF16), w_ref[0:M_W, :])
         + _dot(ma_ref[...].astype(BF16), w_ref[M_W:M_W + A_W, :]))
    out_ref[...] = x_ref[...] + y * lax.rsqrt(jnp.mean(y * y, axis=-1, keepdims=True) + EPS) * g_ref[...]


def _outproj(mixed_m, mixed_a, x, w_out, g_post, *, tm):
    rows = x.shape[0]
    assert rows % tm == 0
    return pl.pallas_call(
        _outproj_kernel,
        grid=(rows // tm,),
        in_specs=[
            pl.BlockSpec((tm, M_W), lambda i: (i, 0)),
            pl.BlockSpec((tm, A_W), lambda i: (i, 0)),
            pl.BlockSpec((tm, D_MODEL), lambda i: (i, 0)),
            pl.BlockSpec((M_W + A_W, D_MODEL), lambda i: (0, 0)),
            pl.BlockSpec((1, D_MODEL), lambda i: (0, 0)),
        ],
        out_specs=pl.BlockSpec((tm, D_MODEL), lambda i: (i, 0)),
        out_shape=jax.ShapeDtypeStruct((rows, D_MODEL), F32),
        compiler_params=pltpu.CompilerParams(
            dimension_semantics=("arbitrary",), vmem_limit_bytes=V7X_VMEM_LIMIT),
        name="outproj",
    )(mixed_m, mixed_a, x, w_out, g_post)


def _rope_tables(pos):
    half = A_DK // 2
    inv = ROPE_THETA ** (-jnp.arange(half, dtype=F32) * 2.0 / A_DK)
    ang = pos.astype(F32)[:, None] * inv[None, :]
    cos, sin = jnp.cos(ang), jnp.sin(ang)
    reps = A_DV // A_DK
    return (jnp.tile(jnp.concatenate([cos, cos], axis=1), (1, reps)),
            jnp.tile(jnp.concatenate([-sin, sin], axis=1), (1, reps)))


def _row_tile(rows, want):
    return math.gcd(rows, want)


def _layer(x, pos, state, attend, weights, l, *, batch, seq, chunk, slab_dtype, mix_dtype):
    (w_main, wg, wgt, bias_row, bias_col, w_out, g_pre, g_post, g_mlstm) = weights
    rows = batch * seq
    x2 = x.reshape(rows, D_MODEL)
    cos, sin = _rope_tables(pos)
    slab, k_rot, v_new, fslab, gates, gatest = _inproj(
        x2, g_pre, w_main, wg, wgt, cos, sin, tm=_row_tile(rows, 512), slab_dtype=slab_dtype)
    nc = seq // chunk
    gatest3 = gatest.reshape(8, batch * nc, chunk).transpose(1, 0, 2)
    mixed_m, c_new, n_new, m_new = _mlstm(
        slab, fslab, gates, gatest3, bias_row, bias_col, g_mlstm, state,
        batch=batch, seq=seq, L=chunk, out_dtype=mix_dtype)
    mixed_a = attend(slab, k_rot, v_new, fslab)
    y = _outproj(mixed_m, mixed_a, x2, w_out, g_post, tm=_row_tile(rows, 512))
    return (y.reshape(batch, seq, D_MODEL),
            k_rot.reshape(batch, seq, A_H, 2 * A_DK), v_new.reshape(batch, seq, A_H, A_DV),
            c_new, n_new.reshape(batch, M_H, M_D), m_new.reshape(batch, M_H))


def kernel(x_prompt, x_sample, cache_k, cache_v, state_C, state_n, state_m, page_table, w_in, b_i, b_f, w_out, g_pre, g_post, g_mlstm, g_diff, lam_q1, lam_k1, lam_q2, lam_k2):
    depth = w_in.shape[0]
    B, T, _ = x_prompt.shape
    DB, TS, _ = x_sample.shape
    n_past = page_table.shape[1] * PAGE
    pos_p = jnp.arange(T)
    pos_s = jnp.tile(n_past + jnp.arange(TS), DB)
    assert B == 1

    xp, xs = x_prompt, x_sample
    outs = [[] for _ in range(10)]
    for l in range(depth):
        lam_init = 0.8 - 0.6 * math.exp(-0.3 * l)
        wl = w_in[l]
        sizes = [M_W] * 5 + [M_H, M_H] + [A_W] * 4
        offs = [0]
        for s in sizes:
            offs.append(offs[-1] + s)
        col = lambda n: wl[:, offs[n]:offs[n + 1]]
        mq, mk, mv, mo, mz, wig, wfg, aq, ak, av, az = [col(n) for n in range(11)]
        w_main = jnp.concatenate([mq, mk, mv, aq, ak, av, mo, mz, az], axis=1).astype(BF16)
        wg8 = jnp.concatenate([wig, wfg], axis=1)
        wg = jnp.pad(wg8, ((0, 0), (0, GATE_PAD - 2 * M_H)))
        wgt = wg8.T
        bias8 = jnp.concatenate([b_i[l], b_f[l]])
        bias_row = jnp.pad(bias8, (0, GATE_PAD - 2 * M_H)).reshape(1, GATE_PAD)
        bias_col = bias8.reshape(8, 1)
        weights = (w_main, wg, wgt, bias_row, bias_col, w_out[l].astype(BF16),
                   g_pre[l].reshape(1, D_MODEL), g_post[l].reshape(1, D_MODEL), g_mlstm[l].reshape(1, M_W))
        gd = g_diff[l].reshape(1, A_W)
        lam_params = jnp.stack([lam_q1[l], lam_k1[l], lam_q2[l], lam_k2[l]])

        def prompt_attend(slab, k_rot, v_new, fslab):
            return _prompt_attn(slab, fslab, gd, lam_params, seq=T, tq=_row_tile(T, 512), lam_init=lam_init)

        def sample_attend(slab, k_rot, v_new, fslab):
            return _sample_attn(page_table, slab, k_rot, v_new, fslab, cache_k[l], cache_v[l], gd, lam_params,
                                batch=DB, ts=TS, lam_init=lam_init)

        xp, kk, vv, C, n, m = _layer(xp, pos_p, None, prompt_attend, weights, l,
                                     batch=B, seq=T, chunk=math.gcd(T, 256), slab_dtype=BF16, mix_dtype=BF16)
        for lst, a in zip(outs[:5], (kk, vv, C, n, m)):
            lst.append(a)
        state = (state_C[l], state_n[l].reshape(DB, M_H, 1, M_D), state_m[l].reshape(DB, M_H, 1, 1))
        xs, kk, vv, C, n, m = _layer(xs, pos_s, state, sample_attend, weights, l,
                                     batch=DB, seq=TS, chunk=TS, slab_dtype=F32, mix_dtype=F32)
        for lst, a in zip(outs[5:], (kk, vv, C, n, m)):
            lst.append(a)

    return (xp, xs) + tuple(jnp.stack(o) for o in outs)
```

```python
import functools
import math

import jax
import jax.numpy as jnp
from jax import lax
from jax.experimental import pallas as pl
from jax.experimental.pallas import tpu as pltpu

F32 = jnp.float32
BF16 = jnp.bfloat16

D_MODEL = 2048
M_H = 4
M_D = 256
M_W = M_H * M_D
A_H = 8
A_DV = 128
A_DK = 64
A_W = A_H * A_DV
PAGE = 128
ROPE_THETA = 10000.0
EPS = 1e-6
PIECE = 1024
GATE_PAD = 128
V7X_VMEM_LIMIT = 56 * 1024 * 1024

_P_MQ, _P_MK, _P_MV, _P_AQ, _P_AK, _P_AV, _P_MO, _P_MZ, _P_AZ = range(9)
N_PIECES = 9
N_SLAB = 6
N_FSLAB = 3


def _dot(a, b):
    return jnp.dot(a, b, preferred_element_type=F32)


def _dot_nt(a, b):
    return lax.dot_general(a, b, (((1,), (1,)), ((), ())), preferred_element_type=F32)


def _dot_tn(a, b):
    return lax.dot_general(a, b, (((0,), (0,)), ((), ())), preferred_element_type=F32)


def _split2(a):
    hi = a.astype(BF16)
    lo = (a - hi.astype(F32)).astype(BF16)
    return hi, lo


def _split3(a):
    a1 = a.astype(BF16)
    r1 = a - a1.astype(F32)
    a2 = r1.astype(BF16)
    a3 = (r1 - a2.astype(F32)).astype(BF16)
    return a1, a2, a3


def _sigmoid(x):
    return 1.0 / (1.0 + jnp.exp(-x))


def _log_sigmoid(x):
    return jnp.minimum(x, 0.0) - jnp.log(1.0 + jnp.exp(-jnp.abs(x)))


def _inproj_kernel(x_ref, gpre_ref, w_ref, wg_ref, wgt_ref, cos_ref, sin_ref,
                   slab_ref, k_ref, v_ref, fslab_ref, gates_ref, gatest_ref, h_ref):
    j = pl.program_id(1)
    tm = x_ref.shape[0]

    @pl.when(j == 0)
    def _():
        x = x_ref[...]
        h = x * lax.rsqrt(jnp.mean(x * x, axis=-1, keepdims=True) + EPS) * gpre_ref[...]
        h_hi, h_lo = _split2(h)
        h_ref[...] = h_hi
        wg_hi, wg_lo = _split2(wg_ref[...])
        gates_ref[...] = _dot(h_lo, wg_hi) + _dot(h_hi, wg_lo) + _dot(h_hi, wg_hi)
        wgt_hi, wgt_lo = _split2(wgt_ref[...])
        gatest_ref[...] = _dot_nt(wgt_hi, h_lo) + _dot_nt(wgt_lo, h_hi) + _dot_nt(wgt_hi, h_hi)

    acc = _dot(h_ref[...], w_ref[...])

    def rope_chunks(a):
        cos = cos_ref[...]
        sin = sin_ref[...]
        lane = lax.broadcasted_iota(jnp.int32, (tm, A_DV), 1)
        first_half = (lane % A_DK) < (A_DK // 2)
        for c in range(PIECE // A_DV):
            xc = a[:, c * A_DV:(c + 1) * A_DV]
            rot = jnp.where(first_half, pltpu.roll(xc, A_DV - A_DK // 2, 1), pltpu.roll(xc, A_DK // 2, 1))
            yield c, xc * cos + rot * sin

    @pl.when((j == _P_MQ) | (j == _P_MV))
    def _():
        slab_ref[...] = acc.astype(slab_ref.dtype)

    @pl.when(j == _P_MK)
    def _():
        slab_ref[...] = (acc * (M_D ** -0.5)).astype(slab_ref.dtype)

    @pl.when(j == _P_AQ)
    def _():
        for c, y in rope_chunks(acc):
            slab_ref[:, c * A_DV:(c + 1) * A_DV] = (y * (A_DK ** -0.5)).astype(slab_ref.dtype)

    @pl.when(j == _P_AK)
    def _():
        for c, y in rope_chunks(acc):
            k_ref[:, c * A_DV:(c + 1) * A_DV] = y
            slab_ref[:, c * A_DV:(c + 1) * A_DV] = y.astype(slab_ref.dtype)

    @pl.when(j == _P_AV)
    def _():
        v_ref[...] = acc
        slab_ref[...] = acc.astype(slab_ref.dtype)

    @pl.when(j >= N_SLAB)
    def _():
        fslab_ref[...] = acc


def _inproj(x, g_pre, w_main, wg, wgt, cos, sin, *, tm, slab_dtype):
    rows = x.shape[0]
    assert rows % tm == 0
    grid = (rows // tm, N_PIECES)
    row_blk = lambda width: pl.BlockSpec((tm, width), lambda i, j: (i, 0))
    return pl.pallas_call(
        _inproj_kernel,
        grid=grid,
        in_specs=[
            row_blk(D_MODEL),
            pl.BlockSpec((1, D_MODEL), lambda i, j: (0, 0)),
            pl.BlockSpec((D_MODEL, PIECE), lambda i, j: (0, j)),
            pl.BlockSpec((D_MODEL, GATE_PAD), lambda i, j: (0, 0)),
            pl.BlockSpec((8, D_MODEL), lambda i, j: (0, 0)),
            row_blk(A_DV),
            row_blk(A_DV),
        ],
        out_specs=[
            pl.BlockSpec((tm, PIECE), lambda i, j: (i, jnp.minimum(j, N_SLAB - 1))),
            row_blk(PIECE),
            row_blk(PIECE),
            pl.BlockSpec((tm, PIECE), lambda i, j: (i, jnp.maximum(j - N_SLAB, 0))),
            row_blk(GATE_PAD),
            pl.BlockSpec((8, tm), lambda i, j: (0, i)),
        ],
        out_shape=[
            jax.ShapeDtypeStruct((rows, N_SLAB * PIECE), slab_dtype),
            jax.ShapeDtypeStruct((rows, PIECE), F32),
            jax.ShapeDtypeStruct((rows, PIECE), F32),
            jax.ShapeDtypeStruct((rows, N_FSLAB * PIECE), F32),
            jax.ShapeDtypeStruct((rows, GATE_PAD), F32),
            jax.ShapeDtypeStruct((8, rows), F32),
        ],
        scratch_shapes=[pltpu.VMEM((tm, D_MODEL), BF16)],
        compiler_params=pltpu.CompilerParams(
            dimension_semantics=("arbitrary", "arbitrary"), vmem_limit_bytes=V7X_VMEM_LIMIT),
        name="inproj",
    )(x, g_pre, w_main, wg, wgt, cos, sin)


def _mlstm_kernel(*refs, L, has_state):
    if has_state:
        (q_ref, k_ref, v_ref, mo_ref, mz_ref, g_ref, gt_ref, bias_ref, biast_ref, gm_ref,
         c0_ref, n0_ref, m0_ref, out_ref, cout_ref, nout_ref, mout_ref, c_s, n_s, m_s) = refs
    else:
        (q_ref, k_ref, v_ref, mo_ref, mz_ref, g_ref, gt_ref, bias_ref, biast_ref, gm_ref,
         out_ref, cout_ref, nout_ref, mout_ref, c_s, n_s, m_s) = refs
    hd = pl.program_id(1)
    c = pl.program_id(2)

    @pl.when(c == 0)
    def _():
        if has_state:
            c_s[...] = c0_ref[...]
            n_s[...] = n0_ref[...]
            m_s[...] = m0_ref[...]
        else:
            c_s[...] = jnp.zeros_like(c_s)
            n_s[...] = jnp.zeros_like(n_s)
            m_s[...] = jnp.zeros_like(m_s)

    q = q_ref[...].astype(BF16)
    k = k_ref[...].astype(BF16)
    v = v_ref[...].astype(BF16)

    gcol = g_ref[...] + bias_ref[...]
    grow = gt_ref[...] + biast_ref[...]
    lane = lax.broadcasted_iota(jnp.int32, (L, GATE_PAD), 1)
    sub = lax.broadcasted_iota(jnp.int32, (8, L), 0)
    r_i = lax.broadcasted_iota(jnp.int32, (L, L), 0)
    c_i = lax.broadcasted_iota(jnp.int32, (L, L), 1)
    causal = r_i >= c_i
    lf_col = _log_sigmoid(gcol)
    lf_row = _log_sigmoid(grow)
    if L >= 128:
        tri = jnp.where(causal, 1.0, 0.0).astype(BF16)
        trit = jnp.where(r_i <= c_i, 1.0, 0.0).astype(BF16)
        a1, a2, a3 = _split3(lf_col)
        bcols = _dot(tri, a3) + _dot(tri, a2) + _dot(tri, a1)
        a1, a2, a3 = _split3(lf_row)
        brows = _dot(a3, trit) + _dot(a2, trit) + _dot(a1, trit)
    else:
        rows = [lf_col[0:1, :]]
        for t in range(1, L):
            rows.append(rows[-1] + lf_col[t:t + 1, :])
        bcols = jnp.concatenate(rows, axis=0)
        lane_t = lax.broadcasted_iota(jnp.int32, (8, L), 1)
        brows = jnp.zeros((8, L), F32)
        for s in range(L):
            brows = brows + jnp.where(lane_t >= s, lf_row[:, s:s + 1], 0.0)
    i_col = jnp.sum(jnp.where(lane == hd, gcol, 0.0), axis=1, keepdims=True)
    b_col = jnp.sum(jnp.where(lane == hd + M_H, bcols, 0.0), axis=1, keepdims=True)
    i_row = jnp.sum(jnp.where(sub == hd, grow, 0.0), axis=0, keepdims=True)
    b_row = jnp.sum(jnp.where(sub == hd + M_H, brows, 0.0), axis=0, keepdims=True)

    m_prev = m_s[...]
    logd = jnp.where(causal, b_col - b_row + i_row, -jnp.inf)
    m_col = jnp.maximum(b_col + m_prev, jnp.max(logd, axis=1, keepdims=True))
    dmat = jnp.exp(logd - m_col)
    inter = jnp.exp(b_col + m_prev - m_col)

    cmat = c_s[...]
    nvec = n_s[...]
    w = _dot_nt(q, k) * dmat
    num = _dot(w.astype(BF16), v) + inter * _dot_nt(q, cmat.astype(BF16))
    qf = q.astype(F32)
    den = jnp.sum(w, axis=1, keepdims=True) + inter * jnp.sum(qf * nvec, axis=1, keepdims=True)
    hstate = num / jnp.maximum(jnp.abs(den), jnp.exp(-m_col))

    m_new = m_col[L - 1:L, :]
    b_last = b_col[L - 1:L, :]
    gdec = jnp.exp(b_last - b_col + i_col - m_new)
    decay = jnp.exp(b_last + m_prev - m_new)
    gk = gdec * k.astype(F32)
    c_new = decay * cmat + _dot_tn(v, gk.astype(BF16))
    n_new = decay * nvec + jnp.sum(gk, axis=0, keepdims=True)
    c_s[...] = c_new
    n_s[...] = n_new
    m_s[...] = m_new

    hm = _sigmoid(mo_ref[...]) * hstate
    hm = hm * lax.rsqrt(jnp.mean(hm * hm, axis=-1, keepdims=True) + EPS) * gm_ref[...]
    mz = mz_ref[...]
    out_ref[...] = (hm * (mz * _sigmoid(mz))).astype(out_ref.dtype)

    @pl.when(c == pl.num_programs(2) - 1)
    def _():
        cout_ref[...] = c_new
        nout_ref[...] = n_new
        mout_ref[...] = m_new


def _mlstm(slab, fslab, gates, gatest3, bias_row, bias_col, g_mlstm, state, *, batch, seq, L, out_dtype):
    nc = seq // L
    assert nc * L == seq
    has_state = state is not None
    tok = lambda off: pl.BlockSpec((L, M_D), lambda b, h, c: (b * nc + c, off + h))
    const = lambda shape: pl.BlockSpec(shape, lambda b, h, c: (0,) * len(shape))
    per_bh = lambda r, cdim: pl.BlockSpec((None, None, r, cdim), lambda b, h, c: (b, h, 0, 0))
    in_specs = [
        tok(0 * M_H), tok(1 * M_H), tok(2 * M_H),
        tok(0 * M_H), tok(1 * M_H),
        pl.BlockSpec((L, GATE_PAD), lambda b, h, c: (b * nc + c, 0)),
        pl.BlockSpec((None, 8, L), lambda b, h, c: (b * nc + c, 0, 0)),
        const((1, GATE_PAD)), const((8, 1)),
        pl.BlockSpec((1, M_D), lambda b, h, c: (0, h)),
    ]
    args = [slab, slab, slab, fslab, fslab, gates, gatest3, bias_row, bias_col, g_mlstm]
    if has_state:
        in_specs += [per_bh(M_D, M_D), per_bh(1, M_D), per_bh(1, 1)]
        args += list(state)
    return pl.pallas_call(
        functools.partial(_mlstm_kernel, L=L, has_state=has_state),
        grid=(batch, M_H, nc),
        in_specs=in_specs,
        out_specs=[
            pl.BlockSpec((L, M_D), lambda b, h, c: (b * nc + c, h)),
            per_bh(M_D, M_D), per_bh(1, M_D), per_bh(1, 1),
        ],
        out_shape=[
            jax.ShapeDtypeStruct((batch * seq, M_W), out_dtype),
            jax.ShapeDtypeStruct((batch, M_H, M_D, M_D), F32),
            jax.ShapeDtypeStruct((batch, M_H, 1, M_D), F32),
            jax.ShapeDtypeStruct((batch, M_H, 1, 1), F32),
        ],
        scratch_shapes=[pltpu.VMEM((M_D, M_D), F32), pltpu.VMEM((1, M_D), F32), pltpu.VMEM((1, 1), F32)],
        compiler_params=pltpu.CompilerParams(
            dimension_semantics=("arbitrary", "arbitrary", "arbitrary"), vmem_limit_bytes=V7X_VMEM_LIMIT),
        name="mlstm_state" if has_state else "mlstm_fresh",
    )(*args)


def _lambda(lam_ref, lam_init):
    lam = lam_ref[...]
    s1 = jnp.sum(lam[0:1, :] * lam[1:2, :], axis=1, keepdims=True)
    s2 = jnp.sum(lam[2:3, :] * lam[3:4, :], axis=1, keepdims=True)
    return jnp.exp(s1) - jnp.exp(s2) + lam_init


def _stack_maps(qh):
    lane = lax.broadcasted_iota(jnp.int32, qh.shape, 1)
    zero = jnp.zeros_like(qh)
    return jnp.concatenate([jnp.where(lane < A_DK, qh, zero), jnp.where(lane >= A_DK, qh, zero)], axis=0)


def _online_softmax_step(s, pv_fn, m_s, l_s, acc_s):
    reps = s.shape[1] // A_DV
    m_prev = m_s[...]
    m_next = jnp.maximum(m_prev, jnp.max(s, axis=1, keepdims=True))
    alpha = jnp.exp(m_prev - m_next)
    p = jnp.exp(s - (jnp.tile(m_next, (1, reps)) if reps > 1 else m_next))
    l_s[...] = alpha * l_s[...] + jnp.sum(p, axis=1, keepdims=True)
    acc_s[...] = alpha * acc_s[...] + pv_fn(p.astype(BF16))
    m_s[...] = m_next


def _diff_finish(o, n, lam, gd, az, lam_init):
    ha = o[:n, :] - lam * o[n:, :]
    ha = ha * lax.rsqrt(jnp.mean(ha * ha, axis=-1, keepdims=True) + EPS) * gd * (1.0 - lam_init)
    return ha * (az * _sigmoid(az))


def _prompt_attn_kernel(q_ref, k_ref, v_ref, az_ref, gd_ref, lam_ref, out_ref, m_s, l_s, acc_s, *, tq, lam_init):
    qi = pl.program_id(1)
    qs = _stack_maps(q_ref[...])
    m_s[...] = jnp.full_like(m_s, -jnp.inf)
    l_s[...] = jnp.zeros_like(l_s)
    acc_s[...] = jnp.zeros_like(acc_s)

    def tile(ki, masked):
        start = pl.multiple_of(ki * tq, tq)
        kt = k_ref[pl.ds(start, tq), :]
        vt = v_ref[pl.ds(start, tq), :]
        s = _dot_nt(qs, kt)
        if masked:
            r = lax.broadcasted_iota(jnp.int32, (2 * tq, tq), 0)
            cc = lax.broadcasted_iota(jnp.int32, (2 * tq, tq), 1)
            r = jnp.where(r >= tq, r - tq, r)
            s = jnp.where(r >= cc, s, -jnp.inf)
        _online_softmax_step(s, lambda pb: _dot(pb, vt), m_s, l_s, acc_s)

    def body(ki, carry):
        tile(ki, False)
        return carry

    lax.fori_loop(0, qi, body, 0)
    tile(qi, True)

    lam = _lambda(lam_ref, lam_init)
    out_ref[...] = _diff_finish(acc_s[...] / l_s[...], tq, lam, gd_ref[...], az_ref[...], lam_init).astype(out_ref.dtype)


def _prompt_attn(slab, fslab, g_diff, lam_params, *, seq, tq, lam_init):
    nq = seq // tq
    assert nq * tq == seq
    blk = PIECE // A_DV
    return pl.pallas_call(
        functools.partial(_prompt_attn_kernel, tq=tq, lam_init=lam_init),
        grid=(A_H, nq),
        in_specs=[
            pl.BlockSpec((tq, A_DV), lambda h, i: (i, _P_AQ * blk + h)),
            pl.BlockSpec((seq, A_DV), lambda h, i: (0, _P_AK * blk + h)),
            pl.BlockSpec((seq, A_DV), lambda h, i: (0, _P_AV * blk + h)),
            pl.BlockSpec((tq, A_DV), lambda h, i: (i, (_P_AZ - N_SLAB) * blk + h)),
            pl.BlockSpec((1, A_DV), lambda h, i: (0, h)),
            pl.BlockSpec((4, A_DK), lambda h, i: (0, 0)),
        ],
        out_specs=pl.BlockSpec((tq, A_DV), lambda h, i: (i, h)),
        out_shape=jax.ShapeDtypeStruct((seq, A_W), BF16),
        scratch_shapes=[pltpu.VMEM((2 * tq, A_DV), F32)] * 3,
        compiler_params=pltpu.CompilerParams(
            dimension_semantics=("arbitrary", "arbitrary"), vmem_limit_bytes=V7X_VMEM_LIMIT),
        name="prompt_attn",
    )(slab, slab, slab, fslab, g_diff, lam_params)


def _sample_attn_kernel(pt_ref, q_ref, kn_ref, vn_ref, az_ref, *rest, ts, group, lam_init):
    del pt_ref
    ck_refs, cv_refs = rest[:group], rest[group:2 * group]
    gd_ref, lam_ref, out_ref, qs_s, m_s, l_s, acc_s = rest[2 * group:]
    p = pl.program_id(1)
    n2 = 2 * ts

    @pl.when(p == 0)
    def _():
        q = q_ref[...]
        for h in range(A_H):
            qs_s[h] = _stack_maps(q[:, h * A_DV:(h + 1) * A_DV]).astype(BF16)
        m_s[...] = jnp.full_like(m_s, -jnp.inf)
        l_s[...] = jnp.zeros_like(l_s)
        acc_s[...] = jnp.zeros_like(acc_s)

    def attend(k_tiles, v_tiles, mask=None):
        s = jnp.concatenate(
            [jnp.concatenate([_dot_nt(qs_s[h], kt) for kt in k_tiles[h]], axis=1) for h in range(A_H)], axis=0)
        if mask is not None:
            s = jnp.where(mask, s, -jnp.inf)

        def pv(pb):
            return jnp.concatenate(
                [sum(_dot(pb[h * n2:(h + 1) * n2, g * PAGE:(g + 1) * PAGE], vt) for g, vt in enumerate(v_tiles[h]))
                 for h in range(A_H)], axis=0)

        _online_softmax_step(s, pv, m_s, l_s, acc_s)

    head_rows = lambda ref, h: ref[pl.ds(h, PAGE, stride=A_H), :].astype(BF16)
    attend([[head_rows(r, h) for r in ck_refs] for h in range(A_H)],
           [[head_rows(r, h) for r in cv_refs] for h in range(A_H)])

    @pl.when(p == pl.num_programs(1) - 1)
    def _():
        lam = _lambda(lam_ref, lam_init)
        kn = kn_ref[...]
        vn = vn_ref[...]
        az = az_ref[...]
        gd = gd_ref[...]
        pad = jnp.zeros((PAGE - ts, A_DV), F32)
        head_new = lambda a, h: jnp.concatenate([a[:, h * A_DV:(h + 1) * A_DV], pad], axis=0).astype(BF16)
        r = lax.broadcasted_iota(jnp.int32, (A_H * n2, PAGE), 0)
        cc = lax.broadcasted_iota(jnp.int32, (A_H * n2, PAGE), 1)
        attend([[head_new(kn, h)] for h in range(A_H)], [[head_new(vn, h)] for h in range(A_H)],
               mask=(r % ts) >= cc)
        o = acc_s[...] / l_s[...]
        for h in range(A_H):
            cols = slice(h * A_DV, (h + 1) * A_DV)
            out_ref[:, cols] = _diff_finish(o[h * n2:(h + 1) * n2, :], ts, lam, gd[:, cols], az[:, cols], lam_init)


def _sample_attn(page_table, slab, k_new, v_new, fslab, cache_k, cache_v, g_diff, lam_params, *, batch, ts, lam_init):
    n_pages = page_table.shape[1]
    pool = cache_k.shape[0]
    group = math.gcd(n_pages, 8)
    ck = cache_k.reshape(pool, PAGE * A_H, A_DV)
    cv = cache_v.reshape(pool, PAGE * A_H, A_DV)
    rows = A_H * 2 * ts
    tok = lambda col: pl.BlockSpec((ts, PIECE), lambda b, p, pt: (b, col))
    page = lambda g: pl.BlockSpec((None, PAGE * A_H, A_DV), lambda b, p, pt: (pt[b, p * group + g], 0, 0))
    pages = [page(g) for g in range(group)]
    grid_spec = pltpu.PrefetchScalarGridSpec(
        num_scalar_prefetch=1,
        grid=(batch, n_pages // group),
        in_specs=[tok(_P_AQ), tok(0), tok(0), tok(_P_AZ - N_SLAB)] + pages + pages + [
            pl.BlockSpec((1, A_W), lambda b, p, pt: (0, 0)),
            pl.BlockSpec((4, A_DK), lambda b, p, pt: (0, 0)),
        ],
        out_specs=pl.BlockSpec((ts, A_W), lambda b, p, pt: (b, 0)),
        scratch_shapes=[pltpu.VMEM((A_H, 2 * ts, A_DV), BF16)] + [pltpu.VMEM((rows, A_DV), F32)] * 3,
    )
    return pl.pallas_call(
        functools.partial(_sample_attn_kernel, ts=ts, group=group, lam_init=lam_init),
        grid_spec=grid_spec,
        out_shape=jax.ShapeDtypeStruct((batch * ts, A_W), F32),
        compiler_params=pltpu.CompilerParams(
            dimension_semantics=("arbitrary", "arbitrary"), vmem_limit_bytes=V7X_VMEM_LIMIT),
        name="sample_attn",
    )(page_table, slab, k_new, v_new, fslab, *([ck] * group), *([cv] * group), g_diff, lam_params)


def _outproj_kernel(mm_ref, ma_ref, x_ref, w_ref, g_ref, out_ref):
    y = (_dot(mm_ref[...].astype(BF16), w_ref[0:M_W, :])
         + _dot(ma_ref[...].astype(BF16), w_ref[M_W:M_W + A_W, :]))
    out_ref[...] = x_ref[...] + y * lax.rsqrt(jnp.mean(y * y, axis=-1, keepdims=True) + EPS) * g_ref[...]


def _outproj(mixed_m, mixed_a, x, w_out, g_post, *, tm):
    rows = x.shape[0]
    assert rows % tm == 0
    return pl.pallas_call(
        _outproj_kernel,
        grid=(rows // tm,),
        in_specs=[
            pl.BlockSpec((tm, M_W), lambda i: (i, 0)),
            pl.BlockSpec((tm, A_W), lambda i: (i, 0)),
            pl.BlockSpec((tm, D_MODEL), lambda i: (i, 0)),
            pl.BlockSpec((M_W + A_W, D_MODEL), lambda i: (0, 0)),
            pl.BlockSpec((1, D_MODEL), lambda i: (0, 0)),
        ],
        out_specs=pl.BlockSpec((tm, D_MODEL), lambda i: (i, 0)),
        out_shape=jax.ShapeDtypeStruct((rows, D_MODEL), F32),
        compiler_params=pltpu.CompilerParams(
            dimension_semantics=("arbitrary",), vmem_limit_bytes=V7X_VMEM_LIMIT),
        name="outproj",
    )(mixed_m, mixed_a, x, w_out, g_post)


def _rope_tables(pos):
    half = A_DK // 2
    inv = ROPE_THETA ** (-jnp.arange(half, dtype=F32) * 2.0 / A_DK)
    ang = pos.astype(F32)[:, None] * inv[None, :]
    cos, sin = jnp.cos(ang), jnp.sin(ang)
    reps = A_DV // A_DK
    return (jnp.tile(jnp.concatenate([cos, cos], axis=1), (1, reps)),
            jnp.tile(jnp.concatenate([-sin, sin], axis=1), (1, reps)))


def _row_tile(rows, want):
    return math.gcd(rows, want)


def _layer(x, pos, state, attend, weights, l, *, batch, seq, chunk, slab_dtype, mix_dtype):
    (w_main, wg, wgt, bias_row, bias_col, w_out, g_pre, g_post, g_mlstm) = weights
    rows = batch * seq
    x2 = x.reshape(rows, D_MODEL)
    cos, sin = _rope_tables(pos)
    slab, k_rot, v_new, fslab, gates, gatest = _inproj(
        x2, g_pre, w_main, wg, wgt, cos, sin, tm=_row_tile(rows, 512), slab_dtype=slab_dtype)
    nc = seq // chunk
    gatest3 = gatest.reshape(8, batch * nc, chunk).transpose(1, 0, 2)
    mixed_m, c_new, n_new, m_new = _mlstm(
        slab, fslab, gates, gatest3, bias_row, bias_col, g_mlstm, state,
        batch=batch, seq=seq, L=chunk, out_dtype=mix_dtype)
    mixed_a = attend(slab, k_rot, v_new, fslab)
    y = _outproj(mixed_m, mixed_a, x2, w_out, g_post, tm=_row_tile(rows, 512))
    return (y.reshape(batch, seq, D_MODEL),
            k_rot.reshape(batch, seq, A_H, 2 * A_DK), v_new.reshape(batch, seq, A_H, A_DV),
            c_new, n_new.reshape(batch, M_H, M_D), m_new.reshape(batch, M_H))


def kernel(x_prompt, x_sample, cache_k, cache_v, state_C, state_n, state_m, page_table, w_in, b_i, b_f, w_out, g_pre, g_post, g_mlstm, g_diff, lam_q1, lam_k1, lam_q2, lam_k2):
    depth = w_in.shape[0]
    B, T, _ = x_prompt.shape
    DB, TS, _ = x_sample.shape
    n_past = page_table.shape[1] * PAGE
    pos_p = jnp.arange(T)
    pos_s = jnp.tile(n_past + jnp.arange(TS), DB)
    assert B == 1

    xp, xs = x_prompt, x_sample
    outs = [[] for _ in range(10)]
    for l in range(depth):
        lam_init = 0.8 - 0.6 * math.exp(-0.3 * l)
        wl = w_in[l]
        sizes = [M_W] * 5 + [M_H, M_H] + [A_W] * 4
        offs = [0]
        for s in sizes:
            offs.append(offs[-1] + s)
        col = lambda n: wl[:, offs[n]:offs[n + 1]]
        mq, mk, mv, mo, mz, wig, wfg, aq, ak, av, az = [col(n) for n in range(11)]
        w_main = jnp.concatenate([mq, mk, mv, aq, ak, av, mo, mz, az], axis=1).astype(BF16)
        wg8 = jnp.concatenate([wig, wfg], axis=1)
        wg = jnp.pad(wg8, ((0, 0), (0, GATE_PAD - 2 * M_H)))
        wgt = wg8.T
        bias8 = jnp.concatenate([b_i[l], b_f[l]])
        bias_row = jnp.pad(bias8, (0, GATE_PAD - 2 * M_H)).reshape(1, GATE_PAD)
        bias_col = bias8.reshape(8, 1)
        weights = (w_main, wg, wgt, bias_row, bias_col, w_out[l].astype(BF16),
                   g_pre[l].reshape(1, D_MODEL), g_post[l].reshape(1, D_MODEL), g_mlstm[l].reshape(1, M_W))
        gd = g_diff[l].reshape(1, A_W)
        lam_params = jnp.stack([lam_q1[l], lam_k1[l], lam_q2[l], lam_k2[l]])

        def prompt_attend(slab, k_rot, v_new, fslab):
            return _prompt_attn(slab, fslab, gd, lam_params, seq=T, tq=_row_tile(T, 512), lam_init=lam_init)

        def sample_attend(slab, k_rot, v_new, fslab):
            return _sample_attn(page_table, slab, k_rot, v_new, fslab, cache_k[l], cache_v[l], gd, lam_params,
                                batch=DB, ts=TS, lam_init=lam_init)

        xp, kk, vv, C, n, m = _layer(xp, pos_p, None, prompt_attend, weights, l,
                                     batch=B, seq=T, chunk=math.gcd(T, 256), slab_dtype=BF16, mix_dtype=BF16)
        for lst, a in zip(outs[:5], (kk, vv, C, n, m)):
            lst.append(a)
        state = (state_C[l], state_n[l].reshape(DB, M_H, 1, M_D), state_m[l].reshape(DB, M_H, 1, 1))
        xs, kk, vv, C, n, m = _layer(xs, pos_s, state, sample_attend, weights, l,
                                     batch=DB, seq=TS, chunk=TS, slab_dtype=F32, mix_dtype=F32)
        for lst, a in zip(outs[5:], (kk, vv, C, n, m)):
            lst.append(a)

    return (xp, xs) + tuple(jnp.stack(o) for o in outs)
```

```python
import functools
import math

import jax
import jax.numpy as jnp
from jax import lax
from jax.experimental import pallas as pl
from jax.experimental.pallas import tpu as pltpu

F32 = jnp.float32
BF16 = jnp.bfloat16

D_MODEL = 2048
M_H = 4
M_D = 256
M_W = M_H * M_D
A_H = 8
A_DV = 128
A_DK = 64
A_W = A_H * A_DV
PAGE = 128
ROPE_THETA = 10000.0
EPS = 1e-6
PIECE = 1024
GATE_PAD = 128
ONES_ROWS = 16
V7X_VMEM_LIMIT = 56 * 1024 * 1024
Q_SCALE = A_DK ** -0.5 * math.log2(math.e)

_P_MQ, _P_MK, _P_MV, _P_AQ, _P_AK, _P_AV, _P_MO, _P_MZ, _P_AZ = range(9)
N_PIECES = 9
N_SLAB = 6
N_FSLAB = 3


def _dot(a, b):
    return jnp.dot(a, b, preferred_element_type=F32)


def _dot_nt(a, b):
    return lax.dot_general(a, b, (((1,), (1,)), ((), ())), preferred_element_type=F32)


def _dot_tn(a, b):
    return lax.dot_general(a, b, (((0,), (0,)), ((), ())), preferred_element_type=F32)


def _split2(a):
    hi = a.astype(BF16)
    lo = (a - hi.astype(F32)).astype(BF16)
    return hi, lo


def _split3(a):
    a1 = a.astype(BF16)
    r1 = a - a1.astype(F32)
    a2 = r1.astype(BF16)
    a3 = (r1 - a2.astype(F32)).astype(BF16)
    return a1, a2, a3


def _sigmoid(x):
    return 1.0 / (1.0 + jnp.exp(-x))


def _log_sigmoid(x):
    return jnp.minimum(x, 0.0) - jnp.log(1.0 + jnp.exp(-jnp.abs(x)))


def _inproj_kernel(x_ref, gpre_ref, w_ref, wg_ref, wgt_ref, cos_ref, sin_ref,
                   slab_ref, k_ref, v_ref, fslab_ref, gates_ref, gatest_ref, h_ref):
    j = pl.program_id(1)
    tm = x_ref.shape[0]

    @pl.when(j == 0)
    def _():
        x = x_ref[...]
        h = x * lax.rsqrt(jnp.mean(x * x, axis=-1, keepdims=True) + EPS) * gpre_ref[...]
        h_hi, h_lo = _split2(h)
        h_ref[...] = h_hi
        wg_hi, wg_lo = _split2(wg_ref[...])
        gates_ref[...] = _dot(h_lo, wg_hi) + _dot(h_hi, wg_lo) + _dot(h_hi, wg_hi)
        wgt_hi, wgt_lo = _split2(wgt_ref[...])
        gatest_ref[...] = _dot_nt(wgt_hi, h_lo) + _dot_nt(wgt_lo, h_hi) + _dot_nt(wgt_hi, h_hi)

    acc = _dot(h_ref[...], w_ref[...])

    def rope_chunks(a):
        cos = cos_ref[...]
        sin = sin_ref[...]
        lane = lax.broadcasted_iota(jnp.int32, (tm, A_DV), 1)
        first_half = (lane % A_DK) < (A_DK // 2)
        for c in range(PIECE // A_DV):
            xc = a[:, c * A_DV:(c + 1) * A_DV]
            rot = jnp.where(first_half, pltpu.roll(xc, A_DV - A_DK // 2, 1), pltpu.roll(xc, A_DK // 2, 1))
            yield c, xc * cos + rot * sin

    @pl.when((j == _P_MQ) | (j == _P_MV))
    def _():
        slab_ref[...] = acc.astype(slab_ref.dtype)

    @pl.when(j == _P_MK)
    def _():
        slab_ref[...] = (acc * (M_D ** -0.5)).astype(slab_ref.dtype)

    @pl.when(j == _P_AQ)
    def _():
        for c, y in rope_chunks(acc):
            slab_ref[:, c * A_DV:(c + 1) * A_DV] = (y * Q_SCALE).astype(slab_ref.dtype)

    @pl.when(j == _P_AK)
    def _():
        for c, y in rope_chunks(acc):
            k_ref[:, c * A_DV:(c + 1) * A_DV] = y
            slab_ref[:, c * A_DV:(c + 1) * A_DV] = y.astype(slab_ref.dtype)

    @pl.when(j == _P_AV)
    def _():
        v_ref[...] = acc
        slab_ref[...] = acc.astype(slab_ref.dtype)

    @pl.when(j >= N_SLAB)
    def _():
        fslab_ref[...] = acc


def _inproj(x, g_pre, w_main, wg, wgt, cos, sin, *, tm, slab_dtype):
    rows = x.shape[0]
    assert rows % tm == 0
    grid = (rows // tm, N_PIECES)
    row_blk = lambda width: pl.BlockSpec((tm, width), lambda i, j: (i, 0))
    return pl.pallas_call(
        _inproj_kernel,
        grid=grid,
        in_specs=[
            row_blk(D_MODEL),
            pl.BlockSpec((1, D_MODEL), lambda i, j: (0, 0)),
            pl.BlockSpec((D_MODEL, PIECE), lambda i, j: (0, j)),
            pl.BlockSpec((D_MODEL, GATE_PAD), lambda i, j: (0, 0)),
            pl.BlockSpec((8, D_MODEL), lambda i, j: (0, 0)),
            row_blk(A_DV),
            row_blk(A_DV),
        ],
        out_specs=[
            pl.BlockSpec((tm, PIECE), lambda i, j: (i, jnp.minimum(j, N_SLAB - 1))),
            row_blk(PIECE),
            row_blk(PIECE),
            pl.BlockSpec((tm, PIECE), lambda i, j: (i, jnp.maximum(j - N_SLAB, 0))),
            row_blk(GATE_PAD),
            pl.BlockSpec((8, tm), lambda i, j: (0, i)),
        ],
        out_shape=[
            jax.ShapeDtypeStruct((rows, N_SLAB * PIECE), slab_dtype),
            jax.ShapeDtypeStruct((rows, PIECE), F32),
            jax.ShapeDtypeStruct((rows, PIECE), F32),
            jax.ShapeDtypeStruct((rows, N_FSLAB * PIECE), F32),
            jax.ShapeDtypeStruct((rows, GATE_PAD), F32),
            jax.ShapeDtypeStruct((8, rows), F32),
        ],
        scratch_shapes=[pltpu.VMEM((tm, D_MODEL), BF16)],
        compiler_params=pltpu.CompilerParams(
            dimension_semantics=("arbitrary", "arbitrary"), vmem_limit_bytes=V7X_VMEM_LIMIT),
        name="inproj",
    )(x, g_pre, w_main, wg, wgt, cos, sin)


def _mlstm_kernel(*refs, L, has_state):
    if has_state:
        (q_ref, k_ref, v_ref, mo_ref, mz_ref, g_ref, gt_ref, bias_ref, biast_ref, gm_ref,
         c0_ref, n0_ref, m0_ref, out_ref, cout_ref, nout_ref, mout_ref, c_s, n_s, m_s) = refs
    else:
        (q_ref, k_ref, v_ref, mo_ref, mz_ref, g_ref, gt_ref, bias_ref, biast_ref, gm_ref,
         out_ref, cout_ref, nout_ref, mout_ref, c_s, n_s, m_s) = refs
    c = pl.program_id(1)

    @pl.when(c == 0)
    def _():
        if has_state:
            c_s[...] = c0_ref[...]
            n_s[...] = n0_ref[...]
            m_s[...] = m0_ref[...]
        else:
            c_s[...] = jnp.zeros_like(c_s)
            n_s[...] = jnp.zeros_like(n_s)
            m_s[...] = jnp.zeros_like(m_s)

    gcol = g_ref[...] + bias_ref[...]
    grow = gt_ref[...] + biast_ref[...]
    r_i = lax.broadcasted_iota(jnp.int32, (L, L), 0)
    c_i = lax.broadcasted_iota(jnp.int32, (L, L), 1)
    causal = r_i >= c_i
    lf_col = _log_sigmoid(gcol)
    lf_row = _log_sigmoid(grow)
    if L >= 128:
        tri = jnp.where(causal, 1.0, 0.0).astype(BF16)
        trit = jnp.where(r_i <= c_i, 1.0, 0.0).astype(BF16)
        a1, a2, a3 = _split3(lf_col)
        bcols = _dot(tri, a3) + _dot(tri, a2) + _dot(tri, a1)
        a1, a2, a3 = _split3(lf_row)
        brows = _dot(a3, trit) + _dot(a2, trit) + _dot(a1, trit)
    else:
        rows = [lf_col[0:1, :]]
        for t in range(1, L):
            rows.append(rows[-1] + lf_col[t:t + 1, :])
        bcols = jnp.concatenate(rows, axis=0)
        lane_t = lax.broadcasted_iota(jnp.int32, (8, L), 1)
        brows = jnp.zeros((8, L), F32)
        for s in range(L):
            brows = brows + jnp.where(lane_t >= s, lf_row[:, s:s + 1], 0.0)
    for hd in range(M_H):
        cols = slice(hd * M_D, (hd + 1) * M_D)
        q = q_ref[:, cols].astype(BF16)
        k = k_ref[:, cols].astype(BF16)
        v = v_ref[:, cols].astype(BF16)
        i_col = gcol[:, hd:hd + 1]
        b_col = bcols[:, M_H + hd:M_H + hd + 1]
        i_row = grow[hd:hd + 1, :]
        b_row = brows[M_H + hd:M_H + hd + 1, :]

        m_prev = m_s[hd]
        logd = jnp.where(causal, b_col - b_row + i_row, -jnp.inf)
        m_col = jnp.maximum(b_col + m_prev, jnp.max(logd, axis=1, keepdims=True))
        dmat = jnp.exp(logd - m_col)
        inter = jnp.exp(b_col + m_prev - m_col)

        cmat = c_s[hd]
        nvec = n_s[hd]
        w = _dot_nt(q, k) * dmat
        num = _dot(w.astype(BF16), v) + inter * _dot_nt(q, cmat.astype(BF16))
        qf = q.astype(F32)
        den = jnp.sum(w, axis=1, keepdims=True) + inter * jnp.sum(qf * nvec, axis=1, keepdims=True)
        hstate = num / jnp.maximum(jnp.abs(den), jnp.exp(-m_col))

        m_new = m_col[L - 1:L, :]
        b_last = b_col[L - 1:L, :]
        gdec = jnp.exp(b_last - b_col + i_col - m_new)
        decay = jnp.exp(b_last + m_prev - m_new)
        gk = gdec * k.astype(F32)
        c_s[hd] = decay * cmat + _dot_tn(v, gk.astype(BF16))
        n_s[hd] = decay * nvec + jnp.sum(gk, axis=0, keepdims=True)
        m_s[hd] = m_new

        hm = _sigmoid(mo_ref[:, cols]) * hstate
        hm = hm * lax.rsqrt(jnp.mean(hm * hm, axis=-1, keepdims=True) + EPS) * gm_ref[:, cols]
        mz = mz_ref[:, cols]
        out_ref[:, cols] = (hm * (mz * _sigmoid(mz))).astype(out_ref.dtype)

    @pl.when(c == pl.num_programs(1) - 1)
    def _():
        cout_ref[...] = c_s[...]
        nout_ref[...] = n_s[...]
        mout_ref[...] = m_s[...]


def _mlstm(slab, fslab, gates, gatest3, bias_row, bias_col, g_mlstm, state, *, batch, seq, L, out_dtype):
    nc = seq // L
    assert nc * L == seq
    has_state = state is not None
    tok = lambda piece: pl.BlockSpec((L, M_W), lambda b, c: (b * nc + c, piece))
    const = lambda shape: pl.BlockSpec(shape, lambda b, c: (0,) * len(shape))
    per_b = lambda r, cdim: pl.BlockSpec((None, M_H, r, cdim), lambda b, c: (b, 0, 0, 0))
    in_specs = [
        tok(_P_MQ), tok(_P_MK), tok(_P_MV),
        tok(_P_MO - N_SLAB), tok(_P_MZ - N_SLAB),
        pl.BlockSpec((L, GATE_PAD), lambda b, c: (b * nc + c, 0)),
        pl.BlockSpec((None, 8, L), lambda b, c: (b * nc + c, 0, 0)),
        const((1, GATE_PAD)), const((8, 1)), const((1, M_W)),
    ]
    args = [slab, slab, slab, fslab, fslab, gates, gatest3, bias_row, bias_col, g_mlstm]
    if has_state:
        in_specs += [per_b(M_D, M_D), per_b(1, M_D), per_b(1, 1)]
        args += list(state)
    return pl.pallas_call(
        functools.partial(_mlstm_kernel, L=L, has_state=has_state),
        grid=(batch, nc),
        in_specs=in_specs,
        out_specs=[
            pl.BlockSpec((L, M_W), lambda b, c: (b * nc + c, 0)),
            per_b(M_D, M_D), per_b(1, M_D), per_b(1, 1),
        ],
        out_shape=[
            jax.ShapeDtypeStruct((batch * seq, M_W), out_dtype),
            jax.ShapeDtypeStruct((batch, M_H, M_D, M_D), F32),
            jax.ShapeDtypeStruct((batch, M_H, 1, M_D), F32),
            jax.ShapeDtypeStruct((batch, M_H, 1, 1), F32),
        ],
        scratch_shapes=[pltpu.VMEM((M_H, M_D, M_D), F32), pltpu.VMEM((M_H, 1, M_D), F32), pltpu.VMEM((M_H, 1, 1), F32)],
        compiler_params=pltpu.CompilerParams(
            dimension_semantics=("arbitrary", "arbitrary"), vmem_limit_bytes=V7X_VMEM_LIMIT),
        name="mlstm_state" if has_state else "mlstm_fresh",
    )(*args)


def _lambda(lam_ref, lam_init):
    lam = lam_ref[...]
    s1 = jnp.sum(lam[0:1, :] * lam[1:2, :], axis=1, keepdims=True)
    s2 = jnp.sum(lam[2:3, :] * lam[3:4, :], axis=1, keepdims=True)
    return jnp.exp(s1) - jnp.exp(s2) + lam_init


def _stack_maps(qh):
    lane = lax.broadcasted_iota(jnp.int32, qh.shape, 1)
    zero = jnp.zeros_like(qh)
    return jnp.concatenate([jnp.where(lane < A_DK, qh, zero), jnp.where(lane >= A_DK, qh, zero)], axis=0)


def _online_softmax_step(s, pv_fn, m_s, l_s, acc_s):
    reps = s.shape[1] // A_DV
    m_prev = m_s[...]
    m_next = jnp.maximum(m_prev, jnp.max(s, axis=1, keepdims=True))
    alpha = jnp.exp2(m_prev - m_next)
    p = jnp.exp2(s - (jnp.tile(m_next, (1, reps)) if reps > 1 else m_next))
    l_s[...] = alpha * l_s[...] + jnp.sum(p, axis=1, keepdims=True)
    acc_s[...] = alpha * acc_s[...] + pv_fn(p.astype(BF16))
    m_s[...] = m_next


def _diff_finish(o, n, lam, gd, az, lam_init):
    ha = o[:n, :] - lam * o[n:, :]
    ha = ha * lax.rsqrt(jnp.mean(ha * ha, axis=-1, keepdims=True) + EPS) * gd * (1.0 - lam_init)
    return ha * (az * _sigmoid(az))


def _prompt_attn_kernel(q_ref, k_ref, v_ref, az_ref, gd_ref, lam_ref, out_ref, vt_s, sa_s, sb_s, m_s, acc_s, *, tq, lam_init):
    qi = pl.program_id(1)
    n_tiles = k_ref.shape[0] // tq

    @pl.when(qi == 0)
    def _():
        def xpose(c, carry):
            start = pl.multiple_of(c * tq, tq)
            vt = v_ref[pl.ds(start, tq), :].astype(F32).T.astype(BF16)
            vt_s[c] = jnp.concatenate([vt, jnp.ones((ONES_ROWS, tq), BF16)], axis=0)
            return carry
        lax.fori_loop(0, n_tiles, xpose, 0)

    qs = _stack_maps(q_ref[...])
    m_s[...] = jnp.full_like(m_s, -jnp.inf)
    acc_s[...] = jnp.zeros_like(acc_s)

    def scores(ki, dst):
        start = pl.multiple_of(ki * tq, tq)
        dst[...] = _dot_nt(k_ref[pl.ds(start, tq), :], qs)

    def consume(src, ki, masked):
        st = src[...]
        if masked:
            key = lax.broadcasted_iota(jnp.int32, (tq, 2 * tq), 0)
            qry = lax.broadcasted_iota(jnp.int32, (tq, 2 * tq), 1)
            qry = jnp.where(qry >= tq, qry - tq, qry)
            st = jnp.where(qry >= key, st, -jnp.inf)
        m_prev = m_s[...]
        m_next = jnp.maximum(m_prev, jnp.max(st, axis=0, keepdims=True))
        alpha = jnp.exp2(m_prev - m_next)
        pt = jnp.exp2(st - m_next).astype(BF16)
        acc_s[...] = alpha * acc_s[...] + _dot(vt_s[ki], pt)
        m_s[...] = m_next

    scores(0, sa_s)

    def pair(j, carry):
        scores(2 * j + 1, sb_s)
        consume(sa_s, 2 * j, False)
        scores(2 * j + 2, sa_s)
        consume(sb_s, 2 * j + 1, False)
        return carry

    lax.fori_loop(0, qi // 2, pair, 0)

    @pl.when(qi % 2 == 0)
    def _():
        consume(sa_s, qi, True)

    @pl.when(qi % 2 == 1)
    def _():
        scores(qi, sb_s)
        consume(sa_s, qi - 1, False)
        consume(sb_s, qi, True)

    lam = _lambda(lam_ref, lam_init)
    acc = acc_s[...]
    ot = acc[:A_DV, :] / acc[A_DV:A_DV + 1, :]
    hat = ot[:, :tq] - lam * ot[:, tq:]
    hat = hat * lax.rsqrt(jnp.mean(hat * hat, axis=0, keepdims=True) + EPS)
    az = az_ref[...]
    out_ref[...] = (hat.T * (gd_ref[...] * (1.0 - lam_init)) * (az * _sigmoid(az))).astype(out_ref.dtype)


def _prompt_attn(slab, fslab, g_diff, lam_params, *, seq, tq, lam_init):
    nq = seq // tq
    assert nq * tq == seq
    blk = PIECE // A_DV
    return pl.pallas_call(
        functools.partial(_prompt_attn_kernel, tq=tq, lam_init=lam_init),
        grid=(A_H, nq),
        in_specs=[
            pl.BlockSpec((tq, A_DV), lambda h, i: (i, _P_AQ * blk + h)),
            pl.BlockSpec((seq, A_DV), lambda h, i: (0, _P_AK * blk + h)),
            pl.BlockSpec((seq, A_DV), lambda h, i: (0, _P_AV * blk + h)),
            pl.BlockSpec((tq, A_DV), lambda h, i: (i, (_P_AZ - N_SLAB) * blk + h)),
            pl.BlockSpec((1, A_DV), lambda h, i: (0, h)),
            pl.BlockSpec((4, A_DK), lambda h, i: (0, 0)),
        ],
        out_specs=pl.BlockSpec((tq, A_DV), lambda h, i: (i, h)),
        out_shape=jax.ShapeDtypeStruct((seq, A_W), BF16),
        scratch_shapes=[pltpu.VMEM((nq, A_DV + ONES_ROWS, tq), BF16),
                        pltpu.VMEM((tq, 2 * tq), F32), pltpu.VMEM((tq, 2 * tq), F32),
                        pltpu.VMEM((1, 2 * tq), F32), pltpu.VMEM((A_DV + ONES_ROWS, 2 * tq), F32)],
        compiler_params=pltpu.CompilerParams(
            dimension_semantics=("arbitrary", "arbitrary"), vmem_limit_bytes=V7X_VMEM_LIMIT),
        name="prompt_attn",
    )(slab, slab, slab, fslab, g_diff, lam_params)


def _sample_attn_kernel(pt_ref, q_ref, kn_ref, vn_ref, az_ref, *rest, ts, group, lam_init):
    del pt_ref
    ck_refs, cv_refs = rest[:group], rest[group:2 * group]
    gd_ref, lam_ref, out_ref, qs_s, m_s, l_s, acc_s = rest[2 * group:]
    p = pl.program_id(1)
    n2 = 2 * ts

    @pl.when(p == 0)
    def _():
        q = q_ref[...]
        for h in range(A_H):
            qs_s[h] = _stack_maps(q[:, h * A_DV:(h + 1) * A_DV]).astype(BF16)
        m_s[...] = jnp.full_like(m_s, -jnp.inf)
        l_s[...] = jnp.zeros_like(l_s)
        acc_s[...] = jnp.zeros_like(acc_s)

    def attend(k_tiles, v_tiles, mask=None):
        s = jnp.concatenate(
            [jnp.concatenate([_dot_nt(qs_s[h], kt) for kt in k_tiles[h]], axis=1) for h in range(A_H)], axis=0)
        if mask is not None:
            s = jnp.where(mask, s, -jnp.inf)

        def pv(pb):
            return jnp.concatenate(
                [sum(_dot(pb[h * n2:(h + 1) * n2, g * PAGE:(g + 1) * PAGE], vt) for g, vt in enumerate(v_tiles[h]))
                 for h in range(A_H)], axis=0)

        _online_softmax_step(s, pv, m_s, l_s, acc_s)

    head_rows = lambda ref, h: ref[pl.ds(h, PAGE, stride=A_H), :].astype(BF16)
    attend([[head_rows(r, h) for r in ck_refs] for h in range(A_H)],
           [[head_rows(r, h) for r in cv_refs] for h in range(A_H)])

    @pl.when(p == pl.num_programs(1) - 1)
    def _():
        lam = _lambda(lam_ref, lam_init)
        kn = kn_ref[...]
        vn = vn_ref[...]
        az = az_ref[...]
        gd = gd_ref[...]
        pad = jnp.zeros((PAGE - ts, A_DV), F32)
        head_new = lambda a, h: jnp.concatenate([a[:, h * A_DV:(h + 1) * A_DV], pad], axis=0).astype(BF16)
        r = lax.broadcasted_iota(jnp.int32, (A_H * n2, PAGE), 0)
        cc = lax.broadcasted_iota(jnp.int32, (A_H * n2, PAGE), 1)
        attend([[head_new(kn, h)] for h in range(A_H)], [[head_new(vn, h)] for h in range(A_H)],
               mask=(r % ts) >= cc)
        o = acc_s[...] / l_s[...]
        for h in range(A_H):
            cols = slice(h * A_DV, (h + 1) * A_DV)
            out_ref[:, cols] = _diff_finish(o[h * n2:(h + 1) * n2, :], ts, lam, gd[:, cols], az[:, cols], lam_init)


def _sample_attn(page_table, slab, k_new, v_new, fslab, cache_k, cache_v, g_diff, lam_params, *, batch, ts, lam_init):
    n_pages = page_table.shape[1]
    pool = cache_k.shape[0]
    group = math.gcd(n_pages, 8)
    ck = cache_k.reshape(pool, PAGE * A_H, A_DV)
    cv = cache_v.reshape(pool, PAGE * A_H, A_DV)
    rows = A_H * 2 * ts
    tok = lambda col: pl.BlockSpec((ts, PIECE), lambda b, p, pt: (b, col))
    page = lambda g: pl.BlockSpec((None, PAGE * A_H, A_DV), lambda b, p, pt: (pt[b, p * group + g], 0, 0))
    pages = [page(g) for g in range(group)]
    grid_spec = pltpu.PrefetchScalarGridSpec(
        num_scalar_prefetch=1,
        grid=(batch, n_pages // group),
        in_specs=[tok(_P_AQ), tok(0), tok(0), tok(_P_AZ - N_SLAB)] + pages + pages + [
            pl.BlockSpec((1, A_W), lambda b, p, pt: (0, 0)),
            pl.BlockSpec((4, A_DK), lambda b, p, pt: (0, 0)),
        ],
        out_specs=pl.BlockSpec((ts, A_W), lambda b, p, pt: (b, 0)),
        scratch_shapes=[pltpu.VMEM((A_H, 2 * ts, A_DV), BF16)] + [pltpu.VMEM((rows, A_DV), F32)] * 3,
    )
    return pl.pallas_call(
        functools.partial(_sample_attn_kernel, ts=ts, group=group, lam_init=lam_init),
        grid_spec=grid_spec,
        out_shape=jax.ShapeDtypeStruct((batch * ts, A_W), F32),
        compiler_params=pltpu.CompilerParams(
            dimension_semantics=("arbitrary", "arbitrary"), vmem_limit_bytes=V7X_VMEM_LIMIT),
        name="sample_attn",
    )(page_table, slab, k_new, v_new, fslab, *([ck] * group), *([cv] * group), g_diff, lam_params)


def _outproj_kernel(mm_ref, ma_ref, x_ref, w_ref, g_ref, out_ref):
    y = (_dot(mm_ref[...].astype(BF16), w_ref[0:M_W, :])
         + _dot(ma_ref[...].astype(BF16), w_ref[M_W:M_W + A_W, :]))
    out_ref[...] = x_ref[...] + y * lax.rsqrt(jnp.mean(y * y, axis=-1, keepdims=True) + EPS) * g_ref[...]


def _outproj(mixed_m, mixed_a, x, w_out, g_post, *, tm):
    rows = x.shape[0]
    assert rows % tm == 0
    return pl.pallas_call(
        _outproj_kernel,
        grid=(rows // tm,),
        in_specs=[
            pl.BlockSpec((tm, M_W), lambda i: (i, 0)),
            pl.BlockSpec((tm, A_W), lambda i: (i, 0)),
            pl.BlockSpec((tm, D_MODEL), lambda i: (i, 0)),
            pl.BlockSpec((M_W + A_W, D_MODEL), lambda i: (0, 0)),
            pl.BlockSpec((1, D_MODEL), lambda i: (0, 0)),
        ],
        out_specs=pl.BlockSpec((tm, D_MODEL), lambda i: (i, 0)),
        out_shape=jax.ShapeDtypeStruct((rows, D_MODEL), F32),
        compiler_params=pltpu.CompilerParams(
            dimension_semantics=("arbitrary",), vmem_limit_bytes=V7X_VMEM_LIMIT),
        name="outproj",
    )(mixed_m, mixed_a, x, w_out, g_post)


def _rope_tables(pos):
    half = A_DK // 2
    inv = ROPE_THETA ** (-jnp.arange(half, dtype=F32) * 2.0 / A_DK)
    ang = pos.astype(F32)[:, None] * inv[None, :]
    cos, sin = jnp.cos(ang), jnp.sin(ang)
    reps = A_DV // A_DK
    return (jnp.tile(jnp.concatenate([cos, cos], axis=1), (1, reps)),
            jnp.tile(jnp.concatenate([-sin, sin], axis=1), (1, reps)))


def _row_tile(rows, want):
    return math.gcd(rows, want)


def _layer(x, pos, state, attend, weights, l, *, batch, seq, chunk, slab_dtype, mix_dtype):
    (w_main, wg, wgt, bias_row, bias_col, w_out, g_pre, g_post, g_mlstm) = weights
    rows = batch * seq
    x2 = x.reshape(rows, D_MODEL)
    cos, sin = _rope_tables(pos)
    slab, k_rot, v_new, fslab, gates, gatest = _inproj(
        x2, g_pre, w_main, wg, wgt, cos, sin, tm=_row_tile(rows, 512), slab_dtype=slab_dtype)
    nc = seq // chunk
    gatest3 = gatest.reshape(8, batch * nc, chunk).transpose(1, 0, 2)
    mixed_m, c_new, n_new, m_new = _mlstm(
        slab, fslab, gates, gatest3, bias_row, bias_col, g_mlstm, state,
        batch=batch, seq=seq, L=chunk, out_dtype=mix_dtype)
    mixed_a = attend(slab, k_rot, v_new, fslab)
    y = _outproj(mixed_m, mixed_a, x2, w_out, g_post, tm=_row_tile(rows, 512))
    return (y.reshape(batch, seq, D_MODEL),
            k_rot.reshape(batch, seq, A_H, 2 * A_DK), v_new.reshape(batch, seq, A_H, A_DV),
            c_new, n_new.reshape(batch, M_H, M_D), m_new.reshape(batch, M_H))


def kernel(x_prompt, x_sample, cache_k, cache_v, state_C, state_n, state_m, page_table, w_in, b_i, b_f, w_out, g_pre, g_post, g_mlstm, g_diff, lam_q1, lam_k1, lam_q2, lam_k2):
    depth = w_in.shape[0]
    B, T, _ = x_prompt.shape
    DB, TS, _ = x_sample.shape
    n_past = page_table.shape[1] * PAGE
    pos_p = jnp.arange(T)
    pos_s = jnp.tile(n_past + jnp.arange(TS), DB)
    assert B == 1

    xp, xs = x_prompt, x_sample
    outs = [[] for _ in range(10)]
    for l in range(depth):
        lam_init = 0.8 - 0.6 * math.exp(-0.3 * l)
        wl = w_in[l]
        sizes = [M_W] * 5 + [M_H, M_H] + [A_W] * 4
        offs = [0]
        for s in sizes:
            offs.append(offs[-1] + s)
        col = lambda n: wl[:, offs[n]:offs[n + 1]]
        mq, mk, mv, mo, mz, wig, wfg, aq, ak, av, az = [col(n) for n in range(11)]
        w_main = jnp.concatenate([mq, mk, mv, aq, ak, av, mo, mz, az], axis=1).astype(BF16)
        wg8 = jnp.concatenate([wig, wfg], axis=1)
        wg = jnp.pad(wg8, ((0, 0), (0, GATE_PAD - 2 * M_H)))
        wgt = wg8.T
        bias8 = jnp.concatenate([b_i[l], b_f[l]])
        bias_row = jnp.pad(bias8, (0, GATE_PAD - 2 * M_H)).reshape(1, GATE_PAD)
        bias_col = bias8.reshape(8, 1)
        weights = (w_main, wg, wgt, bias_row, bias_col, w_out[l].astype(BF16),
                   g_pre[l].reshape(1, D_MODEL), g_post[l].reshape(1, D_MODEL), g_mlstm[l].reshape(1, M_W))
        gd = g_diff[l].reshape(1, A_W)
        lam_params = jnp.stack([lam_q1[l], lam_k1[l], lam_q2[l], lam_k2[l]])

        def prompt_attend(slab, k_rot, v_new, fslab):
            return _prompt_attn(slab, fslab, gd, lam_params, seq=T, tq=_row_tile(T, 512), lam_init=lam_init)

        def sample_attend(slab, k_rot, v_new, fslab):
            return _sample_attn(page_table, slab, k_rot, v_new, fslab, cache_k[l], cache_v[l], gd, lam_params,
                                batch=DB, ts=TS, lam_init=lam_init)

        xp, kk, vv, C, n, m = _layer(xp, pos_p, None, prompt_attend, weights, l,
                                     batch=B, seq=T, chunk=math.gcd(T, 256), slab_dtype=BF16, mix_dtype=BF16)
        for lst, a in zip(outs[:5], (kk, vv, C, n, m)):
            lst.append(a)
        state = (state_C[l], state_n[l].reshape(DB, M_H, 1, M_D), state_m[l].reshape(DB, M_H, 1, 1))
        xs, kk, vv, C, n, m = _layer(xs, pos_s, state, sample_attend, weights, l,
                                     batch=DB, seq=TS, chunk=TS, slab_dtype=F32, mix_dtype=F32)
        for lst, a in zip(outs[5:], (kk, vv, C, n, m)):
            lst.append(a)

    return (xp, xs) + tuple(jnp.stack(o) for o in outs)
```

```python
import functools
import math

import jax
import jax.numpy as jnp
from jax import lax
from jax.experimental import pallas as pl
from jax.experimental.pallas import tpu as pltpu

F32 = jnp.float32
BF16 = jnp.bfloat16

D_MODEL = 2048
M_H = 4
M_D = 256
M_W = M_H * M_D
A_H = 8
A_DV = 128
A_DK = 64
A_W = A_H * A_DV
PAGE = 128
ROPE_THETA = 10000.0
EPS = 1e-6
PIECE = 1024
LANES = 128
N_GATES = 2 * M_H
REPACK_W = 512
GATE_PAD = LANES
ONES_ROWS = 16
V7X_VMEM_LIMIT = 56 * 1024 * 1024
Q_SCALE = A_DK ** -0.5 * math.log2(math.e)

_P_MQ, _P_MK, _P_MV, _P_AQ, _P_AK, _P_AV, _P_MO, _P_MZ, _P_AZ = range(9)
N_PIECES = 9
N_SLAB = 6
N_FSLAB = 3


def _dot(a, b):
    return jnp.dot(a, b, preferred_element_type=F32)


def _dot_nt(a, b):
    return lax.dot_general(a, b, (((1,), (1,)), ((), ())), preferred_element_type=F32)


def _dot_tn(a, b):
    return lax.dot_general(a, b, (((0,), (0,)), ((), ())), preferred_element_type=F32)


def _split2(a):
    hi = a.astype(BF16)
    lo = (a - hi.astype(F32)).astype(BF16)
    return hi, lo


def _split3(a):
    a1 = a.astype(BF16)
    r1 = a - a1.astype(F32)
    a2 = r1.astype(BF16)
    a3 = (r1 - a2.astype(F32)).astype(BF16)
    return a1, a2, a3


def _sigmoid(x):
    return 1.0 / (1.0 + jnp.exp(-x))


def _log_sigmoid(x):
    return jnp.minimum(x, 0.0) - jnp.log(1.0 + jnp.exp(-jnp.abs(x)))


def _repack_kernel(lo_idx, hi_idx, shifted, lo_ref, hi_ref, out_ref):
    del lo_idx, hi_idx
    j = pl.program_id(0)

    @pl.when(shifted[j] == 0)
    def _():
        out_ref[...] = lo_ref[...].astype(BF16)

    @pl.when(shifted[j] == 1)
    def _():
        lane = lax.broadcasted_iota(jnp.int32, (lo_ref.shape[0], LANES), 1)
        keep = lane < LANES - N_GATES
        n_chunks = REPACK_W // LANES
        for c in range(n_chunks):
            cur = lo_ref[:, c * LANES:(c + 1) * LANES]
            nxt = lo_ref[:, (c + 1) * LANES:(c + 2) * LANES] if c + 1 < n_chunks else hi_ref[...]
            out_ref[:, c * LANES:(c + 1) * LANES] = jnp.where(
                keep, pltpu.roll(cur, LANES - N_GATES, 1), pltpu.roll(nxt, LANES - N_GATES, 1)).astype(BF16)


def _repack_w_in(wl):
    gate_col = 5 * M_W
    a_col = gate_col + N_GATES
    src = {_P_MQ: 0, _P_MK: M_W, _P_MV: 2 * M_W, _P_MO: 3 * M_W, _P_MZ: 4 * M_W,
           _P_AQ: a_col, _P_AK: a_col + A_W, _P_AV: a_col + 2 * A_W, _P_AZ: a_col + 3 * A_W}
    lo, hi, sh = [], [], []
    for piece in range(N_PIECES):
        for k in range(PIECE // REPACK_W):
            col = src[piece] + k * REPACK_W
            shift = col % LANES
            assert shift in (0, N_GATES) and (col - shift) % REPACK_W == 0
            lo.append((col - shift) // REPACK_W)
            hi.append((col - shift + REPACK_W) // LANES if shift else (hi[-1] if hi else 0))
            sh.append(1 if shift else 0)
    as_i32 = lambda v: jnp.asarray(v, jnp.int32)
    grid_spec = pltpu.PrefetchScalarGridSpec(
        num_scalar_prefetch=3,
        grid=(len(lo),),
        in_specs=[pl.BlockSpec((D_MODEL, REPACK_W), lambda j, lo_i, hi_i, s: (0, lo_i[j])),
                  pl.BlockSpec((D_MODEL, LANES), lambda j, lo_i, hi_i, s: (0, hi_i[j]))],
        out_specs=pl.BlockSpec((D_MODEL, REPACK_W), lambda j, lo_i, hi_i, s: (0, j)),
    )
    return pl.pallas_call(
        _repack_kernel,
        grid_spec=grid_spec,
        out_shape=jax.ShapeDtypeStruct((D_MODEL, N_PIECES * PIECE), BF16),
        compiler_params=pltpu.CompilerParams(dimension_semantics=("arbitrary",), vmem_limit_bytes=V7X_VMEM_LIMIT),
        name="repack_w_in",
    )(as_i32(lo), as_i32(hi), as_i32(sh), wl, wl)


def _inproj_kernel(x_ref, gpre_ref, w_ref, wg_ref, cos_ref, sin_ref,
                   slab_ref, k_ref, v_ref, fslab_ref, gates_ref, gatest_ref, h_ref):
    j = pl.program_id(1)
    tm = x_ref.shape[0]

    @pl.when(j == 0)
    def _():
        x = x_ref[...]
        h = x * lax.rsqrt(jnp.mean(x * x, axis=-1, keepdims=True) + EPS) * gpre_ref[...]
        h_hi, h_lo = _split2(h)
        h_ref[...] = h_hi
        wg_hi, wg_lo = _split2(wg_ref[...])
        hi_pass = _dot(h_hi, jnp.concatenate([wg_hi, wg_lo], axis=1))
        gates = _dot(h_lo, wg_hi) + hi_pass[:, GATE_PAD:] + hi_pass[:, :GATE_PAD]
        gates_ref[...] = gates
        gatest_ref[...] = gates.T[0:8, :]

    acc = _dot(h_ref[...], w_ref[...])

    def rope_chunks(a):
        cos = cos_ref[...]
        sin = sin_ref[...]
        lane = lax.broadcasted_iota(jnp.int32, (tm, A_DV), 1)
        first_half = (lane % A_DK) < (A_DK // 2)
        for c in range(PIECE // A_DV):
            xc = a[:, c * A_DV:(c + 1) * A_DV]
            rot = jnp.where(first_half, pltpu.roll(xc, A_DV - A_DK // 2, 1), pltpu.roll(xc, A_DK // 2, 1))
            yield c, xc * cos + rot * sin

    @pl.when((j == _P_MQ) | (j == _P_MV))
    def _():
        slab_ref[...] = acc.astype(slab_ref.dtype)

    @pl.when(j == _P_MK)
    def _():
        slab_ref[...] = (acc * (M_D ** -0.5)).astype(slab_ref.dtype)

    @pl.when(j == _P_AQ)
    def _():
        for c, y in rope_chunks(acc):
            slab_ref[:, c * A_DV:(c + 1) * A_DV] = (y * Q_SCALE).astype(slab_ref.dtype)

    @pl.when(j == _P_AK)
    def _():
        for c, y in rope_chunks(acc):
            k_ref[:, c * A_DV:(c + 1) * A_DV] = y
            slab_ref[:, c * A_DV:(c + 1) * A_DV] = y.astype(slab_ref.dtype)

    @pl.when(j == _P_AV)
    def _():
        v_ref[...] = acc
        slab_ref[...] = acc.astype(slab_ref.dtype)

    @pl.when(j >= N_SLAB)
    def _():
        fslab_ref[...] = acc


def _inproj(x, g_pre, w_main, wg, cos, sin, *, tm, slab_dtype):
    rows = x.shape[0]
    assert rows % tm == 0
    grid = (rows // tm, N_PIECES)
    row_blk = lambda width: pl.BlockSpec((tm, width), lambda i, j: (i, 0))
    return pl.pallas_call(
        _inproj_kernel,
        grid=grid,
        in_specs=[
            row_blk(D_MODEL),
            pl.BlockSpec((1, D_MODEL), lambda i, j: (0, 0)),
            pl.BlockSpec((D_MODEL, PIECE), lambda i, j: (0, j)),
            pl.BlockSpec((D_MODEL, GATE_PAD), lambda i, j: (0, 0)),
            row_blk(A_DV),
            row_blk(A_DV),
        ],
        out_specs=[
            pl.BlockSpec((tm, PIECE), lambda i, j: (i, jnp.minimum(j, N_SLAB - 1))),
            row_blk(PIECE),
            row_blk(PIECE),
            pl.BlockSpec((tm, PIECE), lambda i, j: (i, jnp.maximum(j - N_SLAB, 0))),
            row_blk(GATE_PAD),
            pl.BlockSpec((8, tm), lambda i, j: (0, i)),
        ],
        out_shape=[
            jax.ShapeDtypeStruct((rows, N_SLAB * PIECE), slab_dtype),
            jax.ShapeDtypeStruct((rows, PIECE), F32),
            jax.ShapeDtypeStruct((rows, PIECE), F32),
            jax.ShapeDtypeStruct((rows, N_FSLAB * PIECE), F32),
            jax.ShapeDtypeStruct((rows, GATE_PAD), F32),
            jax.ShapeDtypeStruct((8, rows), F32),
        ],
        scratch_shapes=[pltpu.VMEM((tm, D_MODEL), BF16)],
        compiler_params=pltpu.CompilerParams(
            dimension_semantics=("arbitrary", "arbitrary"), vmem_limit_bytes=V7X_VMEM_LIMIT),
        name="inproj",
    )(x, g_pre, w_main, wg, cos, sin)


def _mlstm_kernel(*refs, L, has_state):
    if has_state:
        (q_ref, k_ref, v_ref, mo_ref, mz_ref, g_ref, gt_ref, bias_ref, biast_ref, gm_ref,
         c0_ref, n0_ref, m0_ref, out_ref, cout_ref, nout_ref, mout_ref, c_s, n_s, m_s) = refs
    else:
        (q_ref, k_ref, v_ref, mo_ref, mz_ref, g_ref, gt_ref, bias_ref, biast_ref, gm_ref,
         out_ref, cout_ref, nout_ref, mout_ref, c_s, n_s, m_s) = refs
    c = pl.program_id(1)

    @pl.when(c == 0)
    def _():
        if has_state:
            c_s[...] = c0_ref[...]
            n_s[...] = n0_ref[...]
            m_s[...] = m0_ref[...]
        else:
            c_s[...] = jnp.zeros_like(c_s)
            n_s[...] = jnp.zeros_like(n_s)
            m_s[...] = jnp.zeros_like(m_s)

    gcol = g_ref[...] + bias_ref[...]
    grow = gt_ref[...] + biast_ref[...]
    r_i = lax.broadcasted_iota(jnp.int32, (L, L), 0)
    c_i = lax.broadcasted_iota(jnp.int32, (L, L), 1)
    causal = r_i >= c_i
    lf_col = _log_sigmoid(gcol)
    lf_row = _log_sigmoid(grow)
    if L >= 128:
        tri = jnp.where(causal, 1.0, 0.0).astype(BF16)
        trit = jnp.where(r_i <= c_i, 1.0, 0.0).astype(BF16)
        a1, a2, a3 = _split3(lf_col)
        bcols = _dot(tri, a3) + _dot(tri, a2) + _dot(tri, a1)
        a1, a2, a3 = _split3(lf_row)
        brows = _dot(a3, trit) + _dot(a2, trit) + _dot(a1, trit)
    else:
        rows = [lf_col[0:1, :]]
        for t in range(1, L):
            rows.append(rows[-1] + lf_col[t:t + 1, :])
        bcols = jnp.concatenate(rows, axis=0)
        lane_t = lax.broadcasted_iota(jnp.int32, (8, L), 1)
        brows = jnp.zeros((8, L), F32)
        for s in range(L):
            brows = brows + jnp.where(lane_t >= s, lf_row[:, s:s + 1], 0.0)
    for hd in range(M_H):
        cols = slice(hd * M_D, (hd + 1) * M_D)
        q = q_ref[:, cols].astype(BF16)
        k = k_ref[:, cols].astype(BF16)
        v = v_ref[:, cols].astype(BF16)
        i_col = gcol[:, hd:hd + 1]
        b_col = bcols[:, M_H + hd:M_H + hd + 1]
        i_row = grow[hd:hd + 1, :]
        b_row = brows[M_H + hd:M_H + hd + 1, :]

        m_prev = m_s[hd]
        logd = jnp.where(causal, b_col - b_row + i_row, -jnp.inf)
        m_col = jnp.maximum(b_col + m_prev, jnp.max(logd, axis=1, keepdims=True))
        dmat = jnp.exp(logd - m_col)
        inter = jnp.exp(b_col + m_prev - m_col)

        cmat = c_s[hd]
        nvec = n_s[hd]
        w = _dot_nt(q, k) * dmat
        num = _dot(w.astype(BF16), v) + inter * _dot_nt(q, cmat.astype(BF16))
        qf = q.astype(F32)
        den = jnp.sum(w, axis=1, keepdims=True) + inter * jnp.sum(qf * nvec, axis=1, keepdims=True)
        hstate = num / jnp.maximum(jnp.abs(den), jnp.exp(-m_col))

        m_new = m_col[L - 1:L, :]
        b_last = b_col[L - 1:L, :]
        gdec = jnp.exp(b_last - b_col + i_col - m_new)
        decay = jnp.exp(b_last + m_prev - m_new)
        gk = gdec * k.astype(F32)
        c_s[hd] = decay * cmat + _dot_tn(v, gk.astype(BF16))
        n_s[hd] = decay * nvec + jnp.sum(gk, axis=0, keepdims=True)
        m_s[hd] = m_new

        hm = _sigmoid(mo_ref[:, cols]) * hstate
        hm = hm * lax.rsqrt(jnp.mean(hm * hm, axis=-1, keepdims=True) + EPS) * gm_ref[:, cols]
        mz = mz_ref[:, cols]
        out_ref[:, cols] = (hm * (mz * _sigmoid(mz))).astype(out_ref.dtype)

    @pl.when(c == pl.num_programs(1) - 1)
    def _():
        cout_ref[...] = c_s[...]
        nout_ref[...] = n_s[...]
        mout_ref[...] = m_s[...]


def _mlstm(slab, fslab, gates, gatest3, bias_row, bias_col, g_mlstm, state, *, batch, seq, L, out_dtype):
    nc = seq // L
    assert nc * L == seq
    has_state = state is not None
    tok = lambda piece: pl.BlockSpec((L, M_W), lambda b, c: (b * nc + c, piece))
    const = lambda shape: pl.BlockSpec(shape, lambda b, c: (0,) * len(shape))
    per_b = lambda r, cdim: pl.BlockSpec((None, M_H, r, cdim), lambda b, c: (b, 0, 0, 0))
    in_specs = [
        tok(_P_MQ), tok(_P_MK), tok(_P_MV),
        tok(_P_MO - N_SLAB), tok(_P_MZ - N_SLAB),
        pl.BlockSpec((L, GATE_PAD), lambda b, c: (b * nc + c, 0)),
        pl.BlockSpec((None, 8, L), lambda b, c: (b * nc + c, 0, 0)),
        const((1, GATE_PAD)), const((8, 1)), const((1, M_W)),
    ]
    args = [slab, slab, slab, fslab, fslab, gates, gatest3, bias_row, bias_col, g_mlstm]
    if has_state:
        in_specs += [per_b(M_D, M_D), per_b(1, M_D), per_b(1, 1)]
        args += list(state)
    return pl.pallas_call(
        functools.partial(_mlstm_kernel, L=L, has_state=has_state),
        grid=(batch, nc),
        in_specs=in_specs,
        out_specs=[
            pl.BlockSpec((L, M_W), lambda b, c: (b * nc + c, 0)),
            per_b(M_D, M_D), per_b(1, M_D), per_b(1, 1),
        ],
        out_shape=[
            jax.ShapeDtypeStruct((batch * seq, M_W), out_dtype),
            jax.ShapeDtypeStruct((batch, M_H, M_D, M_D), F32),
            jax.ShapeDtypeStruct((batch, M_H, 1, M_D), F32),
            jax.ShapeDtypeStruct((batch, M_H, 1, 1), F32),
        ],
        scratch_shapes=[pltpu.VMEM((M_H, M_D, M_D), F32), pltpu.VMEM((M_H, 1, M_D), F32), pltpu.VMEM((M_H, 1, 1), F32)],
        compiler_params=pltpu.CompilerParams(
            dimension_semantics=("arbitrary", "arbitrary"), vmem_limit_bytes=V7X_VMEM_LIMIT),
        name="mlstm_state" if has_state else "mlstm_fresh",
    )(*args)


def _lambda(lam_ref, lam_init):
    lam = lam_ref[...]
    s1 = jnp.sum(lam[0:1, :] * lam[1:2, :], axis=1, keepdims=True)
    s2 = jnp.sum(lam[2:3, :] * lam[3:4, :], axis=1, keepdims=True)
    return jnp.exp(s1) - jnp.exp(s2) + lam_init


def _stack_maps(qh):
    lane = lax.broadcasted_iota(jnp.int32, qh.shape, 1)
    zero = jnp.zeros_like(qh)
    return jnp.concatenate([jnp.where(lane < A_DK, qh, zero), jnp.where(lane >= A_DK, qh, zero)], axis=0)


def _online_softmax_step(s, pv_fn, m_s, l_s, acc_s):
    reps = s.shape[1] // A_DV
    m_prev = m_s[...]
    m_next = jnp.maximum(m_prev, jnp.max(s, axis=1, keepdims=True))
    alpha = jnp.exp2(m_prev - m_next)
    p = jnp.exp2(s - (jnp.tile(m_next, (1, reps)) if reps > 1 else m_next))
    l_s[...] = alpha * l_s[...] + jnp.sum(p, axis=1, keepdims=True)
    acc_s[...] = alpha * acc_s[...] + pv_fn(p.astype(BF16))
    m_s[...] = m_next


def _diff_finish(o, n, lam, gd, az, lam_init):
    ha = o[:n, :] - lam * o[n:, :]
    ha = ha * lax.rsqrt(jnp.mean(ha * ha, axis=-1, keepdims=True) + EPS) * gd * (1.0 - lam_init)
    return ha * (az * _sigmoid(az))


def _prompt_attn_kernel(q_ref, k_ref, v_ref, az_ref, gd_ref, lam_ref, out_ref, vt_s, sa_s, sb_s, m_s, acc_s, *, tq, lam_init):
    qi = pl.program_id(1)
    n_tiles = k_ref.shape[0] // tq

    @pl.when(qi == 0)
    def _():
        def xpose(c, carry):
            start = pl.multiple_of(c * tq, tq)
            vt = v_ref[pl.ds(start, tq), :].astype(F32).T.astype(BF16)
            vt_s[c] = jnp.concatenate([vt, jnp.ones((ONES_ROWS, tq), BF16)], axis=0)
            return carry
        lax.fori_loop(0, n_tiles, xpose, 0)

    qs = _stack_maps(q_ref[...])
    m_s[...] = jnp.full_like(m_s, -jnp.inf)
    acc_s[...] = jnp.zeros_like(acc_s)

    def scores(ki, dst):
        start = pl.multiple_of(ki * tq, tq)
        dst[...] = _dot_nt(k_ref[pl.ds(start, tq), :], qs)

    def consume(src, ki, masked):
        st = src[...]
        if masked:
            key = lax.broadcasted_iota(jnp.int32, (tq, 2 * tq), 0)
            qry = lax.broadcasted_iota(jnp.int32, (tq, 2 * tq), 1)
            qry = jnp.where(qry >= tq, qry - tq, qry)
            st = jnp.where(qry >= key, st, -jnp.inf)
        m_prev = m_s[...]
        m_next = jnp.maximum(m_prev, jnp.max(st, axis=0, keepdims=True))
        alpha = jnp.exp2(m_prev - m_next)
        pt = jnp.exp2(st - m_next).astype(BF16)
        acc_s[...] = alpha * acc_s[...] + _dot(vt_s[ki], pt)
        m_s[...] = m_next

    scores(0, sa_s)

    def pair(j, carry):
        scores(2 * j + 1, sb_s)
        consume(sa_s, 2 * j, False)
        scores(2 * j + 2, sa_s)
        consume(sb_s, 2 * j + 1, False)
        return carry

    lax.fori_loop(0, qi // 2, pair, 0)

    @pl.when(qi % 2 == 0)
    def _():
        consume(sa_s, qi, True)

    @pl.when(qi % 2 == 1)
    def _():
        scores(qi, sb_s)
        consume(sa_s, qi - 1, False)
        consume(sb_s, qi, True)

    lam = _lambda(lam_ref, lam_init)
    acc = acc_s[...]
    ot = acc[:A_DV, :] / acc[A_DV:A_DV + 1, :]
    hat = ot[:, :tq] - lam * ot[:, tq:]
    hat = hat * lax.rsqrt(jnp.mean(hat * hat, axis=0, keepdims=True) + EPS)
    az = az_ref[...]
    out_ref[...] = (hat.T * (gd_ref[...] * (1.0 - lam_init)) * (az * _sigmoid(az))).astype(out_ref.dtype)


def _prompt_attn(slab, fslab, g_diff, lam_params, *, seq, tq, lam_init):
    nq = seq // tq
    assert nq * tq == seq
    blk = PIECE // A_DV
    return pl.pallas_call(
        functools.partial(_prompt_attn_kernel, tq=tq, lam_init=lam_init),
        grid=(A_H, nq),
        in_specs=[
            pl.BlockSpec((tq, A_DV), lambda h, i: (i, _P_AQ * blk + h)),
            pl.BlockSpec((seq, A_DV), lambda h, i: (0, _P_AK * blk + h)),
            pl.BlockSpec((seq, A_DV), lambda h, i: (0, _P_AV * blk + h)),
            pl.BlockSpec((tq, A_DV), lambda h, i: (i, (_P_AZ - N_SLAB) * blk + h)),
            pl.BlockSpec((1, A_DV), lambda h, i: (0, h)),
            pl.BlockSpec((4, A_DK), lambda h, i: (0, 0)),
        ],
        out_specs=pl.BlockSpec((tq, A_DV), lambda h, i: (i, h)),
        out_shape=jax.ShapeDtypeStruct((seq, A_W), BF16),
        scratch_shapes=[pltpu.VMEM((nq, A_DV + ONES_ROWS, tq), BF16),
                        pltpu.VMEM((tq, 2 * tq), F32), pltpu.VMEM((tq, 2 * tq), F32),
                        pltpu.VMEM((1, 2 * tq), F32), pltpu.VMEM((A_DV + ONES_ROWS, 2 * tq), F32)],
        compiler_params=pltpu.CompilerParams(
            dimension_semantics=("arbitrary", "arbitrary"), vmem_limit_bytes=V7X_VMEM_LIMIT),
        name="prompt_attn",
    )(slab, slab, slab, fslab, g_diff, lam_params)


def _sample_attn_kernel(pt_ref, q_ref, kn_ref, vn_ref, az_ref, *rest, ts, group, lam_init):
    del pt_ref
    ck_refs, cv_refs = rest[:group], rest[group:2 * group]
    gd_ref, lam_ref, out_ref, qs_s, m_s, l_s, acc_s = rest[2 * group:]
    p = pl.program_id(1)
    n2 = 2 * ts

    @pl.when(p == 0)
    def _():
        q = q_ref[...]
        for h in range(A_H):
            qs_s[h] = _stack_maps(q[:, h * A_DV:(h + 1) * A_DV]).astype(BF16)
        m_s[...] = jnp.full_like(m_s, -jnp.inf)
        l_s[...] = jnp.zeros_like(l_s)
        acc_s[...] = jnp.zeros_like(acc_s)

    def attend(k_tiles, v_tiles, mask=None):
        s = jnp.concatenate(
            [jnp.concatenate([_dot_nt(qs_s[h], kt) for kt in k_tiles[h]], axis=1) for h in range(A_H)], axis=0)
        if mask is not None:
            s = jnp.where(mask, s, -jnp.inf)

        def pv(pb):
            return jnp.concatenate(
                [sum(_dot(pb[h * n2:(h + 1) * n2, g * PAGE:(g + 1) * PAGE], vt) for g, vt in enumerate(v_tiles[h]))
                 for h in range(A_H)], axis=0)

        _online_softmax_step(s, pv, m_s, l_s, acc_s)

    head_rows = lambda ref, h: ref[pl.ds(h, PAGE, stride=A_H), :].astype(BF16)
    attend([[head_rows(r, h) for r in ck_refs] for h in range(A_H)],
           [[head_rows(r, h) for r in cv_refs] for h in range(A_H)])

    @pl.when(p == pl.num_programs(1) - 1)
    def _():
        lam = _lambda(lam_ref, lam_init)
        kn = kn_ref[...]
        vn = vn_ref[...]
        az = az_ref[...]
        gd = gd_ref[...]
        pad = jnp.zeros((PAGE - ts, A_DV), F32)
        head_new = lambda a, h: jnp.concatenate([a[:, h * A_DV:(h + 1) * A_DV], pad], axis=0).astype(BF16)
        r = lax.broadcasted_iota(jnp.int32, (A_H * n2, PAGE), 0)
        cc = lax.broadcasted_iota(jnp.int32, (A_H * n2, PAGE), 1)
        attend([[head_new(kn, h)] for h in range(A_H)], [[head_new(vn, h)] for h in range(A_H)],
               mask=(r % ts) >= cc)
        o = acc_s[...] / l_s[...]
        for h in range(A_H):
            cols = slice(h * A_DV, (h + 1) * A_DV)
            out_ref[:, cols] = _diff_finish(o[h * n2:(h + 1) * n2, :], ts, lam, gd[:, cols], az[:, cols], lam_init)


def _sample_attn(page_table, slab, k_new, v_new, fslab, cache_k, cache_v, g_diff, lam_params, *, batch, ts, lam_init):
    n_pages = page_table.shape[1]
    pool = cache_k.shape[0]
    group = math.gcd(n_pages, 8)
    ck = cache_k.reshape(pool, PAGE * A_H, A_DV)
    cv = cache_v.reshape(pool, PAGE * A_H, A_DV)
    rows = A_H * 2 * ts
    tok = lambda col: pl.BlockSpec((ts, PIECE), lambda b, p, pt: (b, col))
    page = lambda g: pl.BlockSpec((None, PAGE * A_H, A_DV), lambda b, p, pt: (pt[b, p * group + g], 0, 0))
    pages = [page(g) for g in range(group)]
    grid_spec = pltpu.PrefetchScalarGridSpec(
        num_scalar_prefetch=1,
        grid=(batch, n_pages // group),
        in_specs=[tok(_P_AQ), tok(0), tok(0), tok(_P_AZ - N_SLAB)] + pages + pages + [
            pl.BlockSpec((1, A_W), lambda b, p, pt: (0, 0)),
            pl.BlockSpec((4, A_DK), lambda b, p, pt: (0, 0)),
        ],
        out_specs=pl.BlockSpec((ts, A_W), lambda b, p, pt: (b, 0)),
        scratch_shapes=[pltpu.VMEM((A_H, 2 * ts, A_DV), BF16)] + [pltpu.VMEM((rows, A_DV), F32)] * 3,
    )
    return pl.pallas_call(
        functools.partial(_sample_attn_kernel, ts=ts, group=group, lam_init=lam_init),
        grid_spec=grid_spec,
        out_shape=jax.ShapeDtypeStruct((batch * ts, A_W), F32),
        compiler_params=pltpu.CompilerParams(
            dimension_semantics=("arbitrary", "arbitrary"), vmem_limit_bytes=V7X_VMEM_LIMIT),
        name="sample_attn",
    )(page_table, slab, k_new, v_new, fslab, *([ck] * group), *([cv] * group), g_diff, lam_params)


def _outproj_kernel(mm_ref, ma_ref, x_ref, w_ref, g_ref, out_ref):
    y = (_dot(mm_ref[...].astype(BF16), w_ref[0:M_W, :])
         + _dot(ma_ref[...].astype(BF16), w_ref[M_W:M_W + A_W, :]))
    out_ref[...] = x_ref[...] + y * lax.rsqrt(jnp.mean(y * y, axis=-1, keepdims=True) + EPS) * g_ref[...]


def _outproj(mixed_m, mixed_a, x, w_out, g_post, *, tm):
    rows = x.shape[0]
    assert rows % tm == 0
    return pl.pallas_call(
        _outproj_kernel,
        grid=(rows // tm,),
        in_specs=[
            pl.BlockSpec((tm, M_W), lambda i: (i, 0)),
            pl.BlockSpec((tm, A_W), lambda i: (i, 0)),
            pl.BlockSpec((tm, D_MODEL), lambda i: (i, 0)),
            pl.BlockSpec((M_W + A_W, D_MODEL), lambda i: (0, 0)),
            pl.BlockSpec((1, D_MODEL), lambda i: (0, 0)),
        ],
        out_specs=pl.BlockSpec((tm, D_MODEL), lambda i: (i, 0)),
        out_shape=jax.ShapeDtypeStruct((rows, D_MODEL), F32),
        compiler_params=pltpu.CompilerParams(
            dimension_semantics=("arbitrary",), vmem_limit_bytes=V7X_VMEM_LIMIT),
        name="outproj",
    )(mixed_m, mixed_a, x, w_out, g_post)


def _rope_tables(pos):
    half = A_DK // 2
    inv = ROPE_THETA ** (-jnp.arange(half, dtype=F32) * 2.0 / A_DK)
    ang = pos.astype(F32)[:, None] * inv[None, :]
    cos, sin = jnp.cos(ang), jnp.sin(ang)
    reps = A_DV // A_DK
    return (jnp.tile(jnp.concatenate([cos, cos], axis=1), (1, reps)),
            jnp.tile(jnp.concatenate([-sin, sin], axis=1), (1, reps)))


def _row_tile(rows, want):
    return math.gcd(rows, want)


def _layer(x, pos, state, attend, weights, l, *, batch, seq, chunk, slab_dtype, mix_dtype):
    (w_main, wg, bias_row, bias_col, w_out, g_pre, g_post, g_mlstm) = weights
    rows = batch * seq
    x2 = x.reshape(rows, D_MODEL)
    cos, sin = _rope_tables(pos)
    slab, k_rot, v_new, fslab, gates, gatest = _inproj(
        x2, g_pre, w_main, wg, cos, sin, tm=_row_tile(rows, 512), slab_dtype=slab_dtype)
    nc = seq // chunk
    gatest3 = gatest.reshape(8, batch * nc, chunk).transpose(1, 0, 2)
    mixed_m, c_new, n_new, m_new = _mlstm(
        slab, fslab, gates, gatest3, bias_row, bias_col, g_mlstm, state,
        batch=batch, seq=seq, L=chunk, out_dtype=mix_dtype)
    mixed_a = attend(slab, k_rot, v_new, fslab)
    y = _outproj(mixed_m, mixed_a, x2, w_out, g_post, tm=_row_tile(rows, 512))
    return (y.reshape(batch, seq, D_MODEL),
            k_rot.reshape(batch, seq, A_H, 2 * A_DK), v_new.reshape(batch, seq, A_H, A_DV),
            c_new, n_new.reshape(batch, M_H, M_D), m_new.reshape(batch, M_H))


def kernel(x_prompt, x_sample, cache_k, cache_v, state_C, state_n, state_m, page_table, w_in, b_i, b_f, w_out, g_pre, g_post, g_mlstm, g_diff, lam_q1, lam_k1, lam_q2, lam_k2):
    depth = w_in.shape[0]
    B, T, _ = x_prompt.shape
    DB, TS, _ = x_sample.shape
    n_past = page_table.shape[1] * PAGE
    pos_p = jnp.arange(T)
    pos_s = jnp.tile(n_past + jnp.arange(TS), DB)
    assert B == 1

    xp, xs = x_prompt, x_sample
    outs = [[] for _ in range(10)]
    for l in range(depth):
        lam_init = 0.8 - 0.6 * math.exp(-0.3 * l)
        wl = w_in[l]
        w_main = _repack_w_in(wl)
        wg = jnp.pad(wl[:, 5 * M_W:5 * M_W + N_GATES], ((0, 0), (0, GATE_PAD - N_GATES)))
        bias8 = jnp.concatenate([b_i[l], b_f[l]])
        bias_row = jnp.pad(bias8, (0, GATE_PAD - N_GATES)).reshape(1, GATE_PAD)
        bias_col = bias8.reshape(N_GATES, 1)
        weights = (w_main, wg, bias_row, bias_col, w_out[l].astype(BF16),
                   g_pre[l].reshape(1, D_MODEL), g_post[l].reshape(1, D_MODEL), g_mlstm[l].reshape(1, M_W))
        gd = g_diff[l].reshape(1, A_W)
        lam_params = jnp.stack([lam_q1[l], lam_k1[l], lam_q2[l], lam_k2[l]])

        def prompt_attend(slab, k_rot, v_new, fslab):
            return _prompt_attn(slab, fslab, gd, lam_params, seq=T, tq=_row_tile(T, 512), lam_init=lam_init)

        def sample_attend(slab, k_rot, v_new, fslab):
            return _sample_attn(page_table, slab, k_rot, v_new, fslab, cache_k[l], cache_v[l], gd, lam_params,
                                batch=DB, ts=TS, lam_init=lam_init)

        xp, kk, vv, C, n, m = _layer(xp, pos_p, None, prompt_attend, weights, l,
                                     batch=B, seq=T, chunk=math.gcd(T, 256), slab_dtype=BF16, mix_dtype=BF16)
        for lst, a in zip(outs[:5], (kk, vv, C, n, m)):
            lst.append(a)
        state = (state_C[l], state_n[l].reshape(DB, M_H, 1, M_D), state_m[l].reshape(DB, M_H, 1, 1))
        xs, kk, vv, C, n, m = _layer(xs, pos_s, state, sample_attend, weights, l,
                                     batch=DB, seq=TS, chunk=TS, slab_dtype=F32, mix_dtype=F32)
        for lst, a in zip(outs[5:], (kk, vv, C, n, m)):
            lst.append(a)

    return (xp, xs) + tuple(jnp.stack(o) for o in outs)
```

```python
import functools
import math

import jax
import jax.numpy as jnp
from jax import lax
from jax.experimental import pallas as pl
from jax.experimental.pallas import tpu as pltpu

F32 = jnp.float32
BF16 = jnp.bfloat16

D_MODEL = 2048
M_H = 4
M_D = 256
M_W = M_H * M_D
A_H = 8
A_DV = 128
A_DK = 64
A_W = A_H * A_DV
PAGE = 128
ROPE_THETA = 10000.0
EPS = 1e-6
PIECE = 1024
LANES = 128
MXU_N = 256
N_GATES = 2 * M_H
REPACK_W = 512
GATE_PAD = LANES
HEADS_PER_STEP = 2
SAMPLE_SEQS_PER_STEP = 2
SAMPLE_PAGES_PER_STEP = 8
ONES_ROWS = 16
V7X_VMEM_LIMIT = 56 * 1024 * 1024
Q_SCALE = A_DK ** -0.5 * math.log2(math.e)

_P_MQ, _P_MK, _P_MV, _P_AQ, _P_AK, _P_AV, _P_MO, _P_MZ, _P_AZ = range(9)
N_PIECES = 9
N_SLAB = 6
N_FSLAB = 3


def _dot(a, b):
    return jnp.dot(a, b, preferred_element_type=F32)


def _dot_nt(a, b):
    return lax.dot_general(a, b, (((1,), (1,)), ((), ())), preferred_element_type=F32)


def _dot_tn(a, b):
    return lax.dot_general(a, b, (((0,), (0,)), ((), ())), preferred_element_type=F32)


def _split2(a):
    hi = a.astype(BF16)
    lo = (a - hi.astype(F32)).astype(BF16)
    return hi, lo


def _split3(a):
    a1 = a.astype(BF16)
    r1 = a - a1.astype(F32)
    a2 = r1.astype(BF16)
    a3 = (r1 - a2.astype(F32)).astype(BF16)
    return a1, a2, a3


def _sigmoid(x):
    return 1.0 / (1.0 + jnp.exp(-x))


def _log_sigmoid(x):
    return jnp.minimum(x, 0.0) - jnp.log(1.0 + jnp.exp(-jnp.abs(x)))


def _repack_kernel(row_off, wt_ref, out_ref):
    del row_off
    out_ref[...] = wt_ref[...].T.astype(BF16)


def _repack_w_in(wt):
    gate_row = 5 * M_W
    a_row = gate_row + N_GATES
    src = {_P_MQ: 0, _P_MK: M_W, _P_MV: 2 * M_W, _P_MO: 3 * M_W, _P_MZ: 4 * M_W,
           _P_AQ: a_row, _P_AK: a_row + A_W, _P_AV: a_row + 2 * A_W, _P_AZ: a_row + 3 * A_W}
    offs = [src[piece] + k * REPACK_W for piece in range(N_PIECES) for k in range(PIECE // REPACK_W)]
    assert all(o % 8 == 0 for o in offs)
    grid_spec = pltpu.PrefetchScalarGridSpec(
        num_scalar_prefetch=1,
        grid=(len(offs),),
        in_specs=[pl.BlockSpec((pl.Element(REPACK_W), pl.Element(D_MODEL)),
                               lambda j, off: (pl.multiple_of(off[j], 8), 0))],
        out_specs=pl.BlockSpec((D_MODEL, REPACK_W), lambda j, off: (0, j)),
    )
    return pl.pallas_call(
        _repack_kernel,
        grid_spec=grid_spec,
        out_shape=jax.ShapeDtypeStruct((D_MODEL, N_PIECES * PIECE), BF16),
        compiler_params=pltpu.CompilerParams(dimension_semantics=("arbitrary",), vmem_limit_bytes=V7X_VMEM_LIMIT),
        name="repack_w_in",
    )(jnp.asarray(offs, jnp.int32), wt)


def _gate_cols_kernel(wt_ref, out_ref):
    lane = lax.broadcasted_iota(jnp.int32, out_ref.shape, 1)
    out_ref[...] = jnp.where(lane < N_GATES, wt_ref[...].T, 0.0)


def _gate_cols(wt):
    return pl.pallas_call(
        _gate_cols_kernel,
        grid=(1,),
        in_specs=[pl.BlockSpec((pl.Element(GATE_PAD), pl.Element(D_MODEL)), lambda i: (5 * M_W, 0))],
        out_specs=pl.BlockSpec((D_MODEL, GATE_PAD), lambda i: (0, 0)),
        out_shape=jax.ShapeDtypeStruct((D_MODEL, GATE_PAD), F32),
        name="gate_cols",
    )(wt)


def _inproj_kernel(x_ref, gpre_ref, w_ref, wg_ref, cos_ref, sin_ref,
                   slab_ref, k_ref, v_ref, fslab_ref, gates_ref, gatest_ref, h_ref):
    j = pl.program_id(1)
    tm = x_ref.shape[0]

    @pl.when(j == 0)
    def _():
        x = x_ref[...]
        h = x * lax.rsqrt(jnp.mean(x * x, axis=-1, keepdims=True) + EPS) * gpre_ref[...]
        h_hi, h_lo = _split2(h)
        h_ref[...] = h_hi
        wg_hi, wg_lo = _split2(wg_ref[...])
        hi_pass = _dot(h_hi, jnp.concatenate([wg_hi, wg_lo], axis=1))
        gates = _dot(h_lo, wg_hi) + hi_pass[:, GATE_PAD:] + hi_pass[:, :GATE_PAD]
        gates_ref[...] = gates
        gatest_ref[...] = gates.T[0:8, :]

    def matmul_chunks():
        for c in range(PIECE // MXU_N):
            cols = slice(c * MXU_N, (c + 1) * MXU_N)
            yield cols, _dot(h_ref[...], w_ref[:, cols])

    def rope_chunks():
        cos = cos_ref[...]
        sin = sin_ref[...]
        lane = lax.broadcasted_iota(jnp.int32, (tm, A_DV), 1)
        first_half = (lane % A_DK) < (A_DK // 2)
        for cols, acc in matmul_chunks():
            for c in range(MXU_N // A_DV):
                xc = acc[:, c * A_DV:(c + 1) * A_DV]
                rot = jnp.where(first_half, pltpu.roll(xc, A_DV - A_DK // 2, 1), pltpu.roll(xc, A_DK // 2, 1))
                yield slice(cols.start + c * A_DV, cols.start + (c + 1) * A_DV), xc * cos + rot * sin

    @pl.when((j == _P_MQ) | (j == _P_MV))
    def _():
        for cols, acc in matmul_chunks():
            slab_ref[:, cols] = acc.astype(slab_ref.dtype)

    @pl.when(j == _P_MK)
    def _():
        for cols, acc in matmul_chunks():
            slab_ref[:, cols] = (acc * (M_D ** -0.5)).astype(slab_ref.dtype)

    @pl.when(j == _P_AQ)
    def _():
        for cols, y in rope_chunks():
            slab_ref[:, cols] = (y * Q_SCALE).astype(slab_ref.dtype)

    @pl.when(j == _P_AK)
    def _():
        for cols, y in rope_chunks():
            k_ref[:, cols] = y
            slab_ref[:, cols] = y.astype(slab_ref.dtype)

    @pl.when(j == _P_AV)
    def _():
        for cols, acc in matmul_chunks():
            v_ref[:, cols] = acc
            slab_ref[:, cols] = acc.astype(slab_ref.dtype)

    @pl.when(j >= N_SLAB)
    def _():
        for cols, acc in matmul_chunks():
            fslab_ref[:, cols] = acc


def _inproj(x, g_pre, w_main, wg, cos, sin, *, tm, slab_dtype):
    rows = x.shape[0]
    assert rows % tm == 0
    grid = (rows // tm, N_PIECES)
    row_blk = lambda width: pl.BlockSpec((tm, width), lambda i, j: (i, 0))
    return pl.pallas_call(
        _inproj_kernel,
        grid=grid,
        in_specs=[
            row_blk(D_MODEL),
            pl.BlockSpec((1, D_MODEL), lambda i, j: (0, 0)),
            pl.BlockSpec((D_MODEL, PIECE), lambda i, j: (0, j)),
            pl.BlockSpec((D_MODEL, GATE_PAD), lambda i, j: (0, 0)),
            row_blk(A_DV),
            row_blk(A_DV),
        ],
        out_specs=[
            pl.BlockSpec((tm, PIECE), lambda i, j: (i, jnp.minimum(j, N_SLAB - 1))),
            row_blk(PIECE),
            row_blk(PIECE),
            pl.BlockSpec((tm, PIECE), lambda i, j: (i, jnp.maximum(j - N_SLAB, 0))),
            row_blk(GATE_PAD),
            pl.BlockSpec((8, tm), lambda i, j: (0, i)),
        ],
        out_shape=[
            jax.ShapeDtypeStruct((rows, N_SLAB * PIECE), slab_dtype),
            jax.ShapeDtypeStruct((rows, PIECE), F32),
            jax.ShapeDtypeStruct((rows, PIECE), F32),
            jax.ShapeDtypeStruct((rows, N_FSLAB * PIECE), F32),
            jax.ShapeDtypeStruct((rows, GATE_PAD), F32),
            jax.ShapeDtypeStruct((8, rows), F32),
        ],
        scratch_shapes=[pltpu.VMEM((tm, D_MODEL), BF16)],
        compiler_params=pltpu.CompilerParams(
            dimension_semantics=("arbitrary", "arbitrary"), vmem_limit_bytes=V7X_VMEM_LIMIT),
        name="inproj",
    )(x, g_pre, w_main, wg, cos, sin)


def _mlstm_kernel(*refs, L, has_state):
    if has_state:
        (q_ref, k_ref, v_ref, mo_ref, mz_ref, g_ref, gt_ref, bias_ref, biast_ref, gm_ref,
         c0_ref, n0_ref, m0_ref, out_ref, cout_ref, nout_ref, mout_ref, c_s, n_s, m_s) = refs
    else:
        (q_ref, k_ref, v_ref, mo_ref, mz_ref, g_ref, gt_ref, bias_ref, biast_ref, gm_ref,
         out_ref, cout_ref, nout_ref, mout_ref, c_s, n_s, m_s) = refs
    c = pl.program_id(1)

    @pl.when(c == 0)
    def _():
        if has_state:
            c_s[...] = c0_ref[...]
            n_s[...] = n0_ref[...]
            m_s[...] = m0_ref[...]
        else:
            c_s[...] = jnp.zeros_like(c_s)
            n_s[...] = jnp.zeros_like(n_s)
            m_s[...] = jnp.zeros_like(m_s)

    gcol = g_ref[...] + bias_ref[...]
    grow = gt_ref[...] + biast_ref[...]
    r_i = lax.broadcasted_iota(jnp.int32, (L, L), 0)
    c_i = lax.broadcasted_iota(jnp.int32, (L, L), 1)
    causal = r_i >= c_i
    lf_col = _log_sigmoid(gcol)
    lf_row = _log_sigmoid(grow)
    if L >= 128:
        tri = jnp.where(causal, 1.0, 0.0).astype(BF16)
        trit = jnp.where(r_i <= c_i, 1.0, 0.0).astype(BF16)
        a1, a2, a3 = _split3(lf_col)
        bcols = _dot(tri, a3) + _dot(tri, a2) + _dot(tri, a1)
        a1, a2, a3 = _split3(lf_row)
        brows = _dot(a3, trit) + _dot(a2, trit) + _dot(a1, trit)
    else:
        rows = [lf_col[0:1, :]]
        for t in range(1, L):
            rows.append(rows[-1] + lf_col[t:t + 1, :])
        bcols = jnp.concatenate(rows, axis=0)
        lane_t = lax.broadcasted_iota(jnp.int32, (8, L), 1)
        brows = jnp.zeros((8, L), F32)
        for s in range(L):
            brows = brows + jnp.where(lane_t >= s, lf_row[:, s:s + 1], 0.0)
    for hd in range(M_H):
        cols = slice(hd * M_D, (hd + 1) * M_D)
        q = q_ref[:, cols].astype(BF16)
        k = k_ref[:, cols].astype(BF16)
        v = v_ref[:, cols].astype(BF16)
        i_col = gcol[:, hd:hd + 1]
        b_col = bcols[:, M_H + hd:M_H + hd + 1]
        i_row = grow[hd:hd + 1, :]
        b_row = brows[M_H + hd:M_H + hd + 1, :]

        m_prev = m_s[hd]
        logd = jnp.where(causal, b_col - b_row + i_row, -jnp.inf)
        m_col = jnp.maximum(b_col + m_prev, jnp.max(logd, axis=1, keepdims=True))
        dmat = jnp.exp(logd - m_col)
        inter = jnp.exp(b_col + m_prev - m_col)

        cmat = c_s[hd]
        nvec = n_s[hd]
        w = _dot_nt(q, k) * dmat
        num = _dot(w.astype(BF16), v) + inter * _dot_nt(q, cmat.astype(BF16))
        qf = q.astype(F32)
        den = jnp.sum(w, axis=1, keepdims=True) + inter * jnp.sum(qf * nvec, axis=1, keepdims=True)
        hstate = num / jnp.maximum(jnp.abs(den), jnp.exp(-m_col))

        m_new = m_col[L - 1:L, :]
        b_last = b_col[L - 1:L, :]
        gdec = jnp.exp(b_last - b_col + i_col - m_new)
        decay = jnp.exp(b_last + m_prev - m_new)
        gk = gdec * k.astype(F32)
        c_s[hd] = decay * cmat + _dot_tn(v, gk.astype(BF16))
        n_s[hd] = decay * nvec + jnp.sum(gk, axis=0, keepdims=True)
        m_s[hd] = m_new

        hm = _sigmoid(mo_ref[:, cols]) * hstate
        hm = hm * lax.rsqrt(jnp.mean(hm * hm, axis=-1, keepdims=True) + EPS) * gm_ref[:, cols]
        mz = mz_ref[:, cols]
        out_ref[:, cols] = (hm * (mz * _sigmoid(mz))).astype(out_ref.dtype)

    @pl.when(c == pl.num_programs(1) - 1)
    def _():
        cout_ref[...] = c_s[...]
        nout_ref[...] = n_s[...]
        mout_ref[...] = m_s[...]


def _mlstm(slab, fslab, gates, gatest3, bias_row, bias_col, g_mlstm, state, *, batch, seq, L, out_dtype):
    nc = seq // L
    assert nc * L == seq
    has_state = state is not None
    tok = lambda piece: pl.BlockSpec((L, M_W), lambda b, c: (b * nc + c, piece))
    const = lambda shape: pl.BlockSpec(shape, lambda b, c: (0,) * len(shape))
    per_b = lambda r, cdim: pl.BlockSpec((None, M_H, r, cdim), lambda b, c: (b, 0, 0, 0))
    in_specs = [
        tok(_P_MQ), tok(_P_MK), tok(_P_MV),
        tok(_P_MO - N_SLAB), tok(_P_MZ - N_SLAB),
        pl.BlockSpec((L, GATE_PAD), lambda b, c: (b * nc + c, 0)),
        pl.BlockSpec((None, 8, L), lambda b, c: (b * nc + c, 0, 0)),
        const((1, GATE_PAD)), const((8, 1)), const((1, M_W)),
    ]
    args = [slab, slab, slab, fslab, fslab, gates, gatest3, bias_row, bias_col, g_mlstm]
    if has_state:
        in_specs += [per_b(M_D, M_D), per_b(1, M_D), per_b(1, 1)]
        args += list(state)
    return pl.pallas_call(
        functools.partial(_mlstm_kernel, L=L, has_state=has_state),
        grid=(batch, nc),
        in_specs=in_specs,
        out_specs=[
            pl.BlockSpec((L, M_W), lambda b, c: (b * nc + c, 0)),
            per_b(M_D, M_D), per_b(1, M_D), per_b(1, 1),
        ],
        out_shape=[
            jax.ShapeDtypeStruct((batch * seq, M_W), out_dtype),
            jax.ShapeDtypeStruct((batch, M_H, M_D, M_D), F32),
            jax.ShapeDtypeStruct((batch, M_H, 1, M_D), F32),
            jax.ShapeDtypeStruct((batch, M_H, 1, 1), F32),
        ],
        scratch_shapes=[pltpu.VMEM((M_H, M_D, M_D), F32), pltpu.VMEM((M_H, 1, M_D), F32), pltpu.VMEM((M_H, 1, 1), F32)],
        compiler_params=pltpu.CompilerParams(
            dimension_semantics=("arbitrary", "arbitrary"), vmem_limit_bytes=V7X_VMEM_LIMIT),
        name="mlstm_state" if has_state else "mlstm_fresh",
    )(*args)


def _lambda(lam_ref, lam_init):
    lam = lam_ref[...]
    s1 = jnp.sum(lam[0:1, :] * lam[1:2, :], axis=1, keepdims=True)
    s2 = jnp.sum(lam[2:3, :] * lam[3:4, :], axis=1, keepdims=True)
    return jnp.exp(s1) - jnp.exp(s2) + lam_init


def _stack_maps(qh):
    lane = lax.broadcasted_iota(jnp.int32, qh.shape, 1)
    zero = jnp.zeros_like(qh)
    return jnp.concatenate([jnp.where(lane < A_DK, qh, zero), jnp.where(lane >= A_DK, qh, zero)], axis=0)


def _online_softmax_step(s, pv_fn, m_s, l_s, acc_s):
    reps = s.shape[1] // A_DV
    m_prev = m_s[...]
    m_next = jnp.maximum(m_prev, jnp.max(s, axis=1, keepdims=True))
    alpha = jnp.exp2(m_prev - m_next)
    p = jnp.exp2(s - (jnp.tile(m_next, (1, reps)) if reps > 1 else m_next))
    l_s[...] = alpha * l_s[...] + jnp.sum(p, axis=1, keepdims=True)
    acc_s[...] = alpha * acc_s[...] + pv_fn(p.astype(BF16))
    m_s[...] = m_next


def _diff_finish(o, n, lam, gd, az, lam_init):
    ha = o[:n, :] - lam * o[n:, :]
    ha = ha * lax.rsqrt(jnp.mean(ha * ha, axis=-1, keepdims=True) + EPS) * gd * (1.0 - lam_init)
    return ha * (az * _sigmoid(az))


def _prompt_attn_kernel(q_ref, k_ref, v_ref, az_ref, gd_ref, lam_ref, out_ref, vt_s, sa_s, sb_s, m_s, acc_s, *, tq, lam_init):
    qi = pl.program_id(1)
    n_tiles = k_ref.shape[0] // tq
    heads = range(HEADS_PER_STEP)
    hcols = lambda hh: slice(hh * A_DV, (hh + 1) * A_DV)

    @pl.when(qi == 0)
    def _():
        def xpose(c, carry):
            start = pl.multiple_of(c * tq, tq)
            for hh in heads:
                vt = v_ref[pl.ds(start, tq), hcols(hh)].astype(F32).T.astype(BF16)
                vt_s[hh, c] = jnp.concatenate([vt, jnp.ones((ONES_ROWS, tq), BF16)], axis=0)
            return carry
        lax.fori_loop(0, n_tiles, xpose, 0)

    qs = [_stack_maps(q_ref[:, hcols(hh)]) for hh in heads]
    m_s[...] = jnp.full_like(m_s, -jnp.inf)
    acc_s[...] = jnp.zeros_like(acc_s)

    def scores(ki, dst):
        start = pl.multiple_of(ki * tq, tq)
        for hh in heads:
            dst[hh] = _dot_nt(k_ref[pl.ds(start, tq), hcols(hh)], qs[hh])

    def consume(src, ki, masked):
        for hh in heads:
            st = src[hh]
            if masked:
                key = lax.broadcasted_iota(jnp.int32, (tq, 2 * tq), 0)
                qry = lax.broadcasted_iota(jnp.int32, (tq, 2 * tq), 1)
                qry = jnp.where(qry >= tq, qry - tq, qry)
                st = jnp.where(qry >= key, st, -jnp.inf)
            m_prev = m_s[hh]
            m_next = jnp.maximum(m_prev, jnp.max(st, axis=0, keepdims=True))
            alpha = jnp.exp2(m_prev - m_next)
            pt = jnp.exp2(st - m_next).astype(BF16)
            acc_s[hh] = alpha * acc_s[hh] + _dot(vt_s[hh, ki], pt)
            m_s[hh] = m_next

    scores(0, sa_s)

    def pair(j, carry):
        scores(2 * j + 1, sb_s)
        consume(sa_s, 2 * j, False)
        scores(2 * j + 2, sa_s)
        consume(sb_s, 2 * j + 1, False)
        return carry

    lax.fori_loop(0, qi // 2, pair, 0)

    @pl.when(qi % 2 == 0)
    def _():
        consume(sa_s, qi, True)

    @pl.when(qi % 2 == 1)
    def _():
        scores(qi, sb_s)
        consume(sa_s, qi - 1, False)
        consume(sb_s, qi, True)

    lam = _lambda(lam_ref, lam_init)
    for hh in heads:
        acc = acc_s[hh]
        ot = acc[:A_DV, :] / acc[A_DV:A_DV + 1, :]
        hat = ot[:, :tq] - lam * ot[:, tq:]
        hat = hat * lax.rsqrt(jnp.mean(hat * hat, axis=0, keepdims=True) + EPS)
        az = az_ref[:, hcols(hh)]
        out_ref[:, hcols(hh)] = (hat.T * (gd_ref[:, hcols(hh)] * (1.0 - lam_init))
                                 * (az * _sigmoid(az))).astype(out_ref.dtype)


def _prompt_attn(slab, fslab, g_diff, lam_params, *, seq, tq, lam_init):
    nq = seq // tq
    assert nq * tq == seq
    hp = HEADS_PER_STEP
    width = hp * A_DV
    blk = PIECE // width
    return pl.pallas_call(
        functools.partial(_prompt_attn_kernel, tq=tq, lam_init=lam_init),
        grid=(A_H // hp, nq),
        in_specs=[
            pl.BlockSpec((tq, width), lambda h, i: (i, _P_AQ * blk + h)),
            pl.BlockSpec((seq, width), lambda h, i: (0, _P_AK * blk + h)),
            pl.BlockSpec((seq, width), lambda h, i: (0, _P_AV * blk + h)),
            pl.BlockSpec((tq, width), lambda h, i: (i, (_P_AZ - N_SLAB) * blk + h)),
            pl.BlockSpec((1, width), lambda h, i: (0, h)),
            pl.BlockSpec((4, A_DK), lambda h, i: (0, 0)),
        ],
        out_specs=pl.BlockSpec((tq, width), lambda h, i: (i, h)),
        out_shape=jax.ShapeDtypeStruct((seq, A_W), BF16),
        scratch_shapes=[pltpu.VMEM((hp, nq, A_DV + ONES_ROWS, tq), BF16),
                        pltpu.VMEM((hp, tq, 2 * tq), F32), pltpu.VMEM((hp, tq, 2 * tq), F32),
                        pltpu.VMEM((hp, 1, 2 * tq), F32), pltpu.VMEM((hp, A_DV + ONES_ROWS, 2 * tq), F32)],
        compiler_params=pltpu.CompilerParams(
            dimension_semantics=("arbitrary", "arbitrary"), vmem_limit_bytes=V7X_VMEM_LIMIT),
        name="prompt_attn",
    )(slab, slab, slab, fslab, g_diff, lam_params)


def _sample_attn_kernel(pt_ref, q_ref, kn_ref, vn_ref, az_ref, *rest, ts, seqs, group, lam_init):
    del pt_ref
    n_pg = seqs * group
    ck_refs, cv_refs = rest[:n_pg], rest[n_pg:2 * n_pg]
    gd_ref, lam_ref, out_ref, qs_s, m_s, l_s, acc_s = rest[2 * n_pg:]
    p = pl.program_id(1)
    n2 = 2 * ts

    @pl.when(p == 0)
    def _():
        for sq in range(seqs):
            q = q_ref[sq * ts:(sq + 1) * ts, :]
            for h in range(A_H):
                qs_s[sq, h] = _stack_maps(q[:, h * A_DV:(h + 1) * A_DV]).astype(BF16)
        m_s[...] = jnp.full_like(m_s, -jnp.inf)
        l_s[...] = jnp.zeros_like(l_s)
        acc_s[...] = jnp.zeros_like(acc_s)

    def attend(sq, k_tiles, v_tiles, mask=None):
        s = jnp.concatenate(
            [jnp.concatenate([_dot_nt(qs_s[sq, h], kt) for kt in k_tiles[h]], axis=1) for h in range(A_H)], axis=0)
        if mask is not None:
            s = jnp.where(mask, s, -jnp.inf)

        def pv(pb):
            return jnp.concatenate(
                [sum(_dot(pb[h * n2:(h + 1) * n2, g * PAGE:(g + 1) * PAGE], vt) for g, vt in enumerate(v_tiles[h]))
                 for h in range(A_H)], axis=0)

        _online_softmax_step(s, pv, m_s.at[sq], l_s.at[sq], acc_s.at[sq])

    head_rows = lambda ref, h: ref[pl.ds(h, PAGE, stride=A_H), :].astype(BF16)
    for sq in range(seqs):
        mine = slice(sq * group, (sq + 1) * group)
        attend(sq, [[head_rows(r, h) for r in ck_refs[mine]] for h in range(A_H)],
               [[head_rows(r, h) for r in cv_refs[mine]] for h in range(A_H)])

    @pl.when(p == pl.num_programs(1) - 1)
    def _():
        lam = _lambda(lam_ref, lam_init)
        gd = gd_ref[...]
        pad = jnp.zeros((PAGE - ts, A_DV), F32)
        head_new = lambda a, h: jnp.concatenate([a[:, h * A_DV:(h + 1) * A_DV], pad], axis=0).astype(BF16)
        r = lax.broadcasted_iota(jnp.int32, (A_H * n2, PAGE), 0)
        cc = lax.broadcasted_iota(jnp.int32, (A_H * n2, PAGE), 1)
        for sq in range(seqs):
            rows = slice(sq * ts, (sq + 1) * ts)
            kn = kn_ref[rows, :]
            vn = vn_ref[rows, :]
            az = az_ref[rows, :]
            attend(sq, [[head_new(kn, h)] for h in range(A_H)], [[head_new(vn, h)] for h in range(A_H)],
                   mask=(r % ts) >= cc)
            o = acc_s[sq] / l_s[sq]
            for h in range(A_H):
                cols = slice(h * A_DV, (h + 1) * A_DV)
                out_ref[rows, cols] = _diff_finish(o[h * n2:(h + 1) * n2, :], ts, lam, gd[:, cols], az[:, cols], lam_init)


def _sample_attn(page_table, slab, k_new, v_new, fslab, cache_k, cache_v, g_diff, lam_params, *, batch, ts, lam_init):
    n_pages = page_table.shape[1]
    pool = cache_k.shape[0]
    seqs = math.gcd(batch, SAMPLE_SEQS_PER_STEP)
    group = math.gcd(n_pages, SAMPLE_PAGES_PER_STEP // seqs)
    ck = cache_k.reshape(pool, PAGE * A_H, A_DV)
    cv = cache_v.reshape(pool, PAGE * A_H, A_DV)
    rows = A_H * 2 * ts
    tok = lambda col: pl.BlockSpec((seqs * ts, PIECE), lambda b, p, pt: (b, col))
    page = lambda sq, g: pl.BlockSpec((None, PAGE * A_H, A_DV),
                                      lambda b, p, pt: (pt[b * seqs + sq, p * group + g], 0, 0))
    pages = [page(sq, g) for sq in range(seqs) for g in range(group)]
    grid_spec = pltpu.PrefetchScalarGridSpec(
        num_scalar_prefetch=1,
        grid=(batch // seqs, n_pages // group),
        in_specs=[tok(_P_AQ), tok(0), tok(0), tok(_P_AZ - N_SLAB)] + pages + pages + [
            pl.BlockSpec((1, A_W), lambda b, p, pt: (0, 0)),
            pl.BlockSpec((4, A_DK), lambda b, p, pt: (0, 0)),
        ],
        out_specs=pl.BlockSpec((seqs * ts, A_W), lambda b, p, pt: (b, 0)),
        scratch_shapes=[pltpu.VMEM((seqs, A_H, 2 * ts, A_DV), BF16)] + [pltpu.VMEM((seqs, rows, A_DV), F32)] * 3,
    )
    n_pg = seqs * group
    return pl.pallas_call(
        functools.partial(_sample_attn_kernel, ts=ts, seqs=seqs, group=group, lam_init=lam_init),
        grid_spec=grid_spec,
        out_shape=jax.ShapeDtypeStruct((batch * ts, A_W), F32),
        compiler_params=pltpu.CompilerParams(
            dimension_semantics=("arbitrary", "arbitrary"), vmem_limit_bytes=V7X_VMEM_LIMIT),
        name="sample_attn",
    )(page_table, slab, k_new, v_new, fslab, *([ck] * n_pg), *([cv] * n_pg), g_diff, lam_params)


def _outproj_kernel(mm_ref, ma_ref, x_ref, w_ref, g_ref, out_ref):
    y = (_dot(mm_ref[...].astype(BF16), w_ref[0:M_W, :])
         + _dot(ma_ref[...].astype(BF16), w_ref[M_W:M_W + A_W, :]))
    out_ref[...] = x_ref[...] + y * lax.rsqrt(jnp.mean(y * y, axis=-1, keepdims=True) + EPS) * g_ref[...]


def _outproj(mixed_m, mixed_a, x, w_out, g_post, *, tm):
    rows = x.shape[0]
    assert rows % tm == 0
    return pl.pallas_call(
        _outproj_kernel,
        grid=(rows // tm,),
        in_specs=[
            pl.BlockSpec((tm, M_W), lambda i: (i, 0)),
            pl.BlockSpec((tm, A_W), lambda i: (i, 0)),
            pl.BlockSpec((tm, D_MODEL), lambda i: (i, 0)),
            pl.BlockSpec((M_W + A_W, D_MODEL), lambda i: (0, 0)),
            pl.BlockSpec((1, D_MODEL), lambda i: (0, 0)),
        ],
        out_specs=pl.BlockSpec((tm, D_MODEL), lambda i: (i, 0)),
        out_shape=jax.ShapeDtypeStruct((rows, D_MODEL), F32),
        compiler_params=pltpu.CompilerParams(
            dimension_semantics=("arbitrary",), vmem_limit_bytes=V7X_VMEM_LIMIT),
        name="outproj",
    )(mixed_m, mixed_a, x, w_out, g_post)


def _rope_tables(pos):
    half = A_DK // 2
    inv = ROPE_THETA ** (-jnp.arange(half, dtype=F32) * 2.0 / A_DK)
    ang = pos.astype(F32)[:, None] * inv[None, :]
    cos, sin = jnp.cos(ang), jnp.sin(ang)
    reps = A_DV // A_DK
    return (jnp.tile(jnp.concatenate([cos, cos], axis=1), (1, reps)),
            jnp.tile(jnp.concatenate([-sin, sin], axis=1), (1, reps)))


def _row_tile(rows, want):
    return math.gcd(rows, want)


def _layer(x, pos, state, attend, weights, l, *, batch, seq, chunk, slab_dtype, mix_dtype):
    (w_main, wg, bias_row, bias_col, w_out, g_pre, g_post, g_mlstm) = weights
    rows = batch * seq
    x2 = x.reshape(rows, D_MODEL)
    cos, sin = _rope_tables(pos)
    slab, k_rot, v_new, fslab, gates, gatest = _inproj(
        x2, g_pre, w_main, wg, cos, sin, tm=_row_tile(rows, 512), slab_dtype=slab_dtype)
    nc = seq // chunk
    gatest3 = gatest.reshape(8, batch * nc, chunk).transpose(1, 0, 2)
    mixed_m, c_new, n_new, m_new = _mlstm(
        slab, fslab, gates, gatest3, bias_row, bias_col, g_mlstm, state,
        batch=batch, seq=seq, L=chunk, out_dtype=mix_dtype)
    mixed_a = attend(slab, k_rot, v_new, fslab)
    y = _outproj(mixed_m, mixed_a, x2, w_out, g_post, tm=_row_tile(rows, 512))
    return (y.reshape(batch, seq, D_MODEL),
            k_rot.reshape(batch, seq, A_H, 2 * A_DK), v_new.reshape(batch, seq, A_H, A_DV),
            c_new, n_new.reshape(batch, M_H, M_D), m_new.reshape(batch, M_H))


def kernel(x_prompt, x_sample, cache_k, cache_v, state_C, state_n, state_m, page_table, w_in, b_i, b_f, w_out, g_pre, g_post, g_mlstm, g_diff, lam_q1, lam_k1, lam_q2, lam_k2):
    depth = w_in.shape[0]
    B, T, _ = x_prompt.shape
    DB, TS, _ = x_sample.shape
    n_past = page_table.shape[1] * PAGE
    pos_p = jnp.arange(T)
    pos_s = jnp.tile(n_past + jnp.arange(TS), DB)
    assert B == 1

    xp, xs = x_prompt, x_sample
    outs = [[] for _ in range(10)]
    for l in range(depth):
        lam_init = 0.8 - 0.6 * math.exp(-0.3 * l)
        wt = w_in[l].T
        w_main = _repack_w_in(wt)
        wg = _gate_cols(wt)
        bias8 = jnp.concatenate([b_i[l], b_f[l]])
        bias_row = jnp.pad(bias8, (0, GATE_PAD - N_GATES)).reshape(1, GATE_PAD)
        bias_col = bias8.reshape(N_GATES, 1)
        weights = (w_main, wg, bias_row, bias_col, w_out[l].astype(BF16),
                   g_pre[l].reshape(1, D_MODEL), g_post[l].reshape(1, D_MODEL), g_mlstm[l].reshape(1, M_W))
        gd = g_diff[l].reshape(1, A_W)
        lam_params = jnp.stack([lam_q1[l], lam_k1[l], lam_q2[l], lam_k2[l]])

        def prompt_attend(slab, k_rot, v_new, fslab):
            return _prompt_attn(slab, fslab, gd, lam_params, seq=T, tq=_row_tile(T, 512), lam_init=lam_init)

        def sample_attend(slab, k_rot, v_new, fslab):
            return _sample_attn(page_table, slab, k_rot, v_new, fslab, cache_k[l], cache_v[l], gd, lam_params,
                                batch=DB, ts=TS, lam_init=lam_init)

        xp, kk, vv, C, n, m = _layer(xp, pos_p, None, prompt_attend, weights, l,
                                     batch=B, seq=T, chunk=math.gcd(T, 256), slab_dtype=BF16, mix_dtype=BF16)
        for lst, a in zip(outs[:5], (kk, vv, C, n, m)):
            lst.append(a)
        state = (state_C[l], state_n[l].reshape(DB, M_H, 1, M_D), state_m[l].reshape(DB, M_H, 1, 1))
        xs, kk, vv, C, n, m = _layer(xs, pos_s, state, sample_attend, weights, l,
                                     batch=DB, seq=TS, chunk=TS, slab_dtype=F32, mix_dtype=F32)
        for lst, a in zip(outs[5:], (kk, vv, C, n, m)):
            lst.append(a)

    return (xp, xs) + tuple(jnp.stack(o) for o in outs)
```

```python
import functools
import math

import jax
import jax.numpy as jnp
from jax import lax
from jax.experimental import pallas as pl
from jax.experimental.pallas import tpu as pltpu

F32 = jnp.float32
BF16 = jnp.bfloat16

D_MODEL = 2048
M_H = 4
M_D = 256
M_W = M_H * M_D
A_H = 8
A_DV = 128
A_DK = 64
A_W = A_H * A_DV
PAGE = 128
ROPE_THETA = 10000.0
EPS = 1e-6
PIECE = 1024
LANES = 128
MXU_N = 256
N_GATES = 2 * M_H
REPACK_W = 512
GATE_PAD = LANES
HEADS_PER_STEP = 4
SAMPLE_SEQS_PER_STEP = 2
SAMPLE_PAGES_PER_STEP = 8
ONES_ROWS = 16
V7X_VMEM_LIMIT = 56 * 1024 * 1024
Q_SCALE = A_DK ** -0.5 * math.log2(math.e)

_P_MQ, _P_MK, _P_MV, _P_AQ, _P_AK, _P_AV, _P_MO, _P_MZ, _P_AZ = range(9)
N_PIECES = 9
N_SLAB = 6
N_FSLAB = 3


def _dot(a, b):
    return jnp.dot(a, b, preferred_element_type=F32)


def _dot_nt(a, b):
    return lax.dot_general(a, b, (((1,), (1,)), ((), ())), preferred_element_type=F32)


def _dot_tn(a, b):
    return lax.dot_general(a, b, (((0,), (0,)), ((), ())), preferred_element_type=F32)


def _split2(a):
    hi = a.astype(BF16)
    lo = (a - hi.astype(F32)).astype(BF16)
    return hi, lo


def _split3(a):
    a1 = a.astype(BF16)
    r1 = a - a1.astype(F32)
    a2 = r1.astype(BF16)
    a3 = (r1 - a2.astype(F32)).astype(BF16)
    return a1, a2, a3


def _sigmoid(x):
    return 1.0 / (1.0 + jnp.exp(-x))


def _log_sigmoid(x):
    return jnp.minimum(x, 0.0) - jnp.log(1.0 + jnp.exp(-jnp.abs(x)))


def _repack_kernel(row_off, wt_ref, out_ref):
    del row_off
    out_ref[...] = wt_ref[...].T.astype(BF16)


def _repack_w_in(wt):
    gate_row = 5 * M_W
    a_row = gate_row + N_GATES
    src = {_P_MQ: 0, _P_MK: M_W, _P_MV: 2 * M_W, _P_MO: 3 * M_W, _P_MZ: 4 * M_W,
           _P_AQ: a_row, _P_AK: a_row + A_W, _P_AV: a_row + 2 * A_W, _P_AZ: a_row + 3 * A_W}
    offs = [src[piece] + k * REPACK_W for piece in range(N_PIECES) for k in range(PIECE // REPACK_W)]
    assert all(o % 8 == 0 for o in offs)
    grid_spec = pltpu.PrefetchScalarGridSpec(
        num_scalar_prefetch=1,
        grid=(len(offs),),
        in_specs=[pl.BlockSpec((pl.Element(REPACK_W), pl.Element(D_MODEL)),
                               lambda j, off: (pl.multiple_of(off[j], 8), 0))],
        out_specs=pl.BlockSpec((D_MODEL, REPACK_W), lambda j, off: (0, j)),
    )
    return pl.pallas_call(
        _repack_kernel,
        grid_spec=grid_spec,
        out_shape=jax.ShapeDtypeStruct((D_MODEL, N_PIECES * PIECE), BF16),
        compiler_params=pltpu.CompilerParams(dimension_semantics=("arbitrary",), vmem_limit_bytes=V7X_VMEM_LIMIT),
        name="repack_w_in",
    )(jnp.asarray(offs, jnp.int32), wt)


def _gate_cols_kernel(wt_ref, out_ref):
    lane = lax.broadcasted_iota(jnp.int32, out_ref.shape, 1)
    out_ref[...] = jnp.where(lane < N_GATES, wt_ref[...].T, 0.0)


def _gate_cols(wt):
    return pl.pallas_call(
        _gate_cols_kernel,
        grid=(1,),
        in_specs=[pl.BlockSpec((pl.Element(GATE_PAD), pl.Element(D_MODEL)), lambda i: (5 * M_W, 0))],
        out_specs=pl.BlockSpec((D_MODEL, GATE_PAD), lambda i: (0, 0)),
        out_shape=jax.ShapeDtypeStruct((D_MODEL, GATE_PAD), F32),
        name="gate_cols",
    )(wt)


def _inproj_kernel(x_ref, gpre_ref, w_ref, wg_ref, cos_ref, sin_ref,
                   slab_ref, k_ref, v_ref, fslab_ref, gates_ref, gatest_ref, h_ref):
    j = pl.program_id(1)
    tm = x_ref.shape[0]

    @pl.when(j == 0)
    def _():
        x = x_ref[...]
        h = x * lax.rsqrt(jnp.mean(x * x, axis=-1, keepdims=True) + EPS) * gpre_ref[...]
        h_hi, h_lo = _split2(h)
        h_ref[...] = h_hi
        wg_hi, wg_lo = _split2(wg_ref[...])
        hi_pass = _dot(h_hi, jnp.concatenate([wg_hi, wg_lo], axis=1))
        gates = _dot(h_lo, wg_hi) + hi_pass[:, GATE_PAD:] + hi_pass[:, :GATE_PAD]
        gates_ref[...] = gates
        gatest_ref[...] = gates.T[0:8, :]

    def matmul_chunks():
        for c in range(PIECE // MXU_N):
            cols = slice(c * MXU_N, (c + 1) * MXU_N)
            yield cols, _dot(h_ref[...], w_ref[:, cols])

    def rope_chunks():
        cos = cos_ref[...]
        sin = sin_ref[...]
        lane = lax.broadcasted_iota(jnp.int32, (tm, A_DV), 1)
        first_half = (lane % A_DK) < (A_DK // 2)
        for cols, acc in matmul_chunks():
            for c in range(MXU_N // A_DV):
                xc = acc[:, c * A_DV:(c + 1) * A_DV]
                rot = jnp.where(first_half, pltpu.roll(xc, A_DV - A_DK // 2, 1), pltpu.roll(xc, A_DK // 2, 1))
                yield slice(cols.start + c * A_DV, cols.start + (c + 1) * A_DV), xc * cos + rot * sin

    @pl.when((j == _P_MQ) | (j == _P_MV))
    def _():
        for cols, acc in matmul_chunks():
            slab_ref[:, cols] = acc.astype(slab_ref.dtype)

    @pl.when(j == _P_MK)
    def _():
        for cols, acc in matmul_chunks():
            slab_ref[:, cols] = (acc * (M_D ** -0.5)).astype(slab_ref.dtype)

    @pl.when(j == _P_AQ)
    def _():
        for cols, y in rope_chunks():
            slab_ref[:, cols] = (y * Q_SCALE).astype(slab_ref.dtype)

    @pl.when(j == _P_AK)
    def _():
        for cols, y in rope_chunks():
            k_ref[:, cols] = y
            slab_ref[:, cols] = y.astype(slab_ref.dtype)

    @pl.when(j == _P_AV)
    def _():
        for cols, acc in matmul_chunks():
            v_ref[:, cols] = acc
            slab_ref[:, cols] = acc.astype(slab_ref.dtype)

    @pl.when(j >= N_SLAB)
    def _():
        for cols, acc in matmul_chunks():
            fslab_ref[:, cols] = acc


def _inproj(x, g_pre, w_main, wg, cos, sin, *, tm, slab_dtype):
    rows = x.shape[0]
    assert rows % tm == 0
    grid = (rows // tm, N_PIECES)
    row_blk = lambda width: pl.BlockSpec((tm, width), lambda i, j: (i, 0))
    return pl.pallas_call(
        _inproj_kernel,
        grid=grid,
        in_specs=[
            row_blk(D_MODEL),
            pl.BlockSpec((1, D_MODEL), lambda i, j: (0, 0)),
            pl.BlockSpec((D_MODEL, PIECE), lambda i, j: (0, j)),
            pl.BlockSpec((D_MODEL, GATE_PAD), lambda i, j: (0, 0)),
            row_blk(A_DV),
            row_blk(A_DV),
        ],
        out_specs=[
            pl.BlockSpec((tm, PIECE), lambda i, j: (i, jnp.minimum(j, N_SLAB - 1))),
            row_blk(PIECE),
            row_blk(PIECE),
            pl.BlockSpec((tm, PIECE), lambda i, j: (i, jnp.maximum(j - N_SLAB, 0))),
            row_blk(GATE_PAD),
            pl.BlockSpec((8, tm), lambda i, j: (0, i)),
        ],
        out_shape=[
            jax.ShapeDtypeStruct((rows, N_SLAB * PIECE), slab_dtype),
            jax.ShapeDtypeStruct((rows, PIECE), F32),
            jax.ShapeDtypeStruct((rows, PIECE), F32),
            jax.ShapeDtypeStruct((rows, N_FSLAB * PIECE), F32),
            jax.ShapeDtypeStruct((rows, GATE_PAD), F32),
            jax.ShapeDtypeStruct((8, rows), F32),
        ],
        scratch_shapes=[pltpu.VMEM((tm, D_MODEL), BF16)],
        compiler_params=pltpu.CompilerParams(
            dimension_semantics=("arbitrary", "arbitrary"), vmem_limit_bytes=V7X_VMEM_LIMIT),
        name="inproj",
    )(x, g_pre, w_main, wg, cos, sin)


def _mlstm_kernel(*refs, L, has_state):
    if has_state:
        (q_ref, k_ref, v_ref, mo_ref, mz_ref, g_ref, gt_ref, bias_ref, biast_ref, gm_ref,
         c0_ref, n0_ref, m0_ref, out_ref, cout_ref, nout_ref, mout_ref, c_s, n_s, m_s) = refs
    else:
        (q_ref, k_ref, v_ref, mo_ref, mz_ref, g_ref, gt_ref, bias_ref, biast_ref, gm_ref,
         out_ref, cout_ref, nout_ref, mout_ref, c_s, n_s, m_s) = refs
    c = pl.program_id(1)

    @pl.when(c == 0)
    def _():
        if has_state:
            c_s[...] = c0_ref[...]
            n_s[...] = n0_ref[...]
            m_s[...] = m0_ref[...]
        else:
            c_s[...] = jnp.zeros_like(c_s)
            n_s[...] = jnp.zeros_like(n_s)
            m_s[...] = jnp.zeros_like(m_s)

    gcol = g_ref[...] + bias_ref[...]
    grow = gt_ref[...] + biast_ref[...]
    r_i = lax.broadcasted_iota(jnp.int32, (L, L), 0)
    c_i = lax.broadcasted_iota(jnp.int32, (L, L), 1)
    causal = r_i >= c_i
    lf_col = _log_sigmoid(gcol)
    lf_row = _log_sigmoid(grow)
    if L >= 128:
        tri = jnp.where(causal, 1.0, 0.0).astype(BF16)
        trit = jnp.where(r_i <= c_i, 1.0, 0.0).astype(BF16)
        a1, a2, a3 = _split3(lf_col)
        bcols = _dot(tri, a3) + _dot(tri, a2) + _dot(tri, a1)
        a1, a2, a3 = _split3(lf_row)
        brows = _dot(a3, trit) + _dot(a2, trit) + _dot(a1, trit)
    else:
        rows = [lf_col[0:1, :]]
        for t in range(1, L):
            rows.append(rows[-1] + lf_col[t:t + 1, :])
        bcols = jnp.concatenate(rows, axis=0)
        lane_t = lax.broadcasted_iota(jnp.int32, (8, L), 1)
        brows = jnp.zeros((8, L), F32)
        for s in range(L):
            brows = brows + jnp.where(lane_t >= s, lf_row[:, s:s + 1], 0.0)
    for hd in range(M_H):
        cols = slice(hd * M_D, (hd + 1) * M_D)
        q = q_ref[:, cols].astype(BF16)
        k = k_ref[:, cols].astype(BF16)
        v = v_ref[:, cols].astype(BF16)
        i_col = gcol[:, hd:hd + 1]
        b_col = bcols[:, M_H + hd:M_H + hd + 1]
        i_row = grow[hd:hd + 1, :]
        b_row = brows[M_H + hd:M_H + hd + 1, :]

        m_prev = m_s[hd]
        logd = jnp.where(causal, b_col - b_row + i_row, -jnp.inf)
        m_col = jnp.maximum(b_col + m_prev, jnp.max(logd, axis=1, keepdims=True))
        dmat = jnp.exp(logd - m_col)
        inter = jnp.exp(b_col + m_prev - m_col)

        cmat = c_s[hd]
        nvec = n_s[hd]
        w = _dot_nt(q, k) * dmat
        num = _dot(w.astype(BF16), v) + inter * _dot_nt(q, cmat.astype(BF16))
        qf = q.astype(F32)
        den = jnp.sum(w, axis=1, keepdims=True) + inter * jnp.sum(qf * nvec, axis=1, keepdims=True)
        hstate = num / jnp.maximum(jnp.abs(den), jnp.exp(-m_col))

        m_new = m_col[L - 1:L, :]
        b_last = b_col[L - 1:L, :]
        gdec = jnp.exp(b_last - b_col + i_col - m_new)
        decay = jnp.exp(b_last + m_prev - m_new)
        gk = gdec * k.astype(F32)
        c_s[hd] = decay * cmat + _dot_tn(v, gk.astype(BF16))
        n_s[hd] = decay * nvec + jnp.sum(gk, axis=0, keepdims=True)
        m_s[hd] = m_new

        hm = _sigmoid(mo_ref[:, cols]) * hstate
        hm = hm * lax.rsqrt(jnp.mean(hm * hm, axis=-1, keepdims=True) + EPS) * gm_ref[:, cols]
        mz = mz_ref[:, cols]
        out_ref[:, cols] = (hm * (mz * _sigmoid(mz))).astype(out_ref.dtype)

    @pl.when(c == pl.num_programs(1) - 1)
    def _():
        cout_ref[...] = c_s[...]
        nout_ref[...] = n_s[...]
        mout_ref[...] = m_s[...]


def _mlstm(slab, fslab, gates, gatest3, bias_row, bias_col, g_mlstm, state, *, batch, seq, L, out_dtype):
    nc = seq // L
    assert nc * L == seq
    has_state = state is not None
    tok = lambda piece: pl.BlockSpec((L, M_W), lambda b, c: (b * nc + c, piece))
    const = lambda shape: pl.BlockSpec(shape, lambda b, c: (0,) * len(shape))
    per_b = lambda r, cdim: pl.BlockSpec((None, M_H, r, cdim), lambda b, c: (b, 0, 0, 0))
    in_specs = [
        tok(_P_MQ), tok(_P_MK), tok(_P_MV),
        tok(_P_MO - N_SLAB), tok(_P_MZ - N_SLAB),
        pl.BlockSpec((L, GATE_PAD), lambda b, c: (b * nc + c, 0)),
        pl.BlockSpec((None, 8, L), lambda b, c: (b * nc + c, 0, 0)),
        const((1, GATE_PAD)), const((8, 1)), const((1, M_W)),
    ]
    args = [slab, slab, slab, fslab, fslab, gates, gatest3, bias_row, bias_col, g_mlstm]
    if has_state:
        in_specs += [per_b(M_D, M_D), per_b(1, M_D), per_b(1, 1)]
        args += list(state)
    return pl.pallas_call(
        functools.partial(_mlstm_kernel, L=L, has_state=has_state),
        grid=(batch, nc),
        in_specs=in_specs,
        out_specs=[
            pl.BlockSpec((L, M_W), lambda b, c: (b * nc + c, 0)),
            per_b(M_D, M_D), per_b(1, M_D), per_b(1, 1),
        ],
        out_shape=[
            jax.ShapeDtypeStruct((batch * seq, M_W), out_dtype),
            jax.ShapeDtypeStruct((batch, M_H, M_D, M_D), F32),
            jax.ShapeDtypeStruct((batch, M_H, 1, M_D), F32),
            jax.ShapeDtypeStruct((batch, M_H, 1, 1), F32),
        ],
        scratch_shapes=[pltpu.VMEM((M_H, M_D, M_D), F32), pltpu.VMEM((M_H, 1, M_D), F32), pltpu.VMEM((M_H, 1, 1), F32)],
        compiler_params=pltpu.CompilerParams(
            dimension_semantics=("arbitrary", "arbitrary"), vmem_limit_bytes=V7X_VMEM_LIMIT),
        name="mlstm_state" if has_state else "mlstm_fresh",
    )(*args)


def _lambda(lam_ref, lam_init):
    lam = lam_ref[...]
    s1 = jnp.sum(lam[0:1, :] * lam[1:2, :], axis=1, keepdims=True)
    s2 = jnp.sum(lam[2:3, :] * lam[3:4, :], axis=1, keepdims=True)
    return jnp.exp(s1) - jnp.exp(s2) + lam_init


def _stack_maps(qh):
    lane = lax.broadcasted_iota(jnp.int32, qh.shape, 1)
    zero = jnp.zeros_like(qh)
    return jnp.concatenate([jnp.where(lane < A_DK, qh, zero), jnp.where(lane >= A_DK, qh, zero)], axis=0)


def _online_softmax_step(s, pv_fn, m_s, l_s, acc_s):
    reps = s.shape[1] // A_DV
    m_prev = m_s[...]
    m_next = jnp.maximum(m_prev, jnp.max(s, axis=1, keepdims=True))
    alpha = jnp.exp2(m_prev - m_next)
    p = jnp.exp2(s - (jnp.tile(m_next, (1, reps)) if reps > 1 else m_next))
    l_s[...] = alpha * l_s[...] + jnp.sum(p, axis=1, keepdims=True)
    acc_s[...] = alpha * acc_s[...] + pv_fn(p.astype(BF16))
    m_s[...] = m_next


def _diff_finish(o, n, lam, gd, az, lam_init):
    ha = o[:n, :] - lam * o[n:, :]
    ha = ha * lax.rsqrt(jnp.mean(ha * ha, axis=-1, keepdims=True) + EPS) * gd * (1.0 - lam_init)
    return ha * (az * _sigmoid(az))


def _prompt_attn_kernel(q_ref, k_ref, v_ref, az_ref, gd_ref, lam_ref, out_ref, vt_s, sa_s, sb_s, m_s, acc_s, *, tq, lam_init):
    qi = pl.program_id(1)
    n_tiles = k_ref.shape[0] // tq
    heads = range(HEADS_PER_STEP)
    hcols = lambda hh: slice(hh * A_DV, (hh + 1) * A_DV)

    @pl.when(qi == 0)
    def _():
        def xpose(c, carry):
            start = pl.multiple_of(c * tq, tq)
            for hh in heads:
                vt = v_ref[pl.ds(start, tq), hcols(hh)].astype(F32).T.astype(BF16)
                vt_s[hh, c] = jnp.concatenate([vt, jnp.ones((ONES_ROWS, tq), BF16)], axis=0)
            return carry
        lax.fori_loop(0, n_tiles, xpose, 0)

    qs = [_stack_maps(q_ref[:, hcols(hh)]) for hh in heads]
    m_s[...] = jnp.full_like(m_s, -jnp.inf)
    acc_s[...] = jnp.zeros_like(acc_s)

    def scores(ki, dst):
        start = pl.multiple_of(ki * tq, tq)
        for hh in heads:
            dst[hh] = _dot_nt(k_ref[pl.ds(start, tq), hcols(hh)], qs[hh])

    def consume(src, ki, masked):
        for hh in heads:
            st = src[hh]
            if masked:
                key = lax.broadcasted_iota(jnp.int32, (tq, 2 * tq), 0)
                qry = lax.broadcasted_iota(jnp.int32, (tq, 2 * tq), 1)
                qry = jnp.where(qry >= tq, qry - tq, qry)
                st = jnp.where(qry >= key, st, -jnp.inf)
            m_prev = m_s[hh]
            m_next = jnp.maximum(m_prev, jnp.max(st, axis=0, keepdims=True))
            alpha = jnp.exp2(m_prev - m_next)
            pt = jnp.exp2(st - m_next).astype(BF16)
            acc_s[hh] = alpha * acc_s[hh] + _dot(vt_s[hh, ki], pt)
            m_s[hh] = m_next

    scores(0, sa_s)

    def pair(j, carry):
        scores(2 * j + 1, sb_s)
        consume(sa_s, 2 * j, False)
        scores(2 * j + 2, sa_s)
        consume(sb_s, 2 * j + 1, False)
        return carry

    lax.fori_loop(0, qi // 2, pair, 0)

    @pl.when(qi % 2 == 0)
    def _():
        consume(sa_s, qi, True)

    @pl.when(qi % 2 == 1)
    def _():
        scores(qi, sb_s)
        consume(sa_s, qi - 1, False)
        consume(sb_s, qi, True)

    lam = _lambda(lam_ref, lam_init)
    for hh in heads:
        acc = acc_s[hh]
        ot = acc[:A_DV, :] / acc[A_DV:A_DV + 1, :]
        hat = ot[:, :tq] - lam * ot[:, tq:]
        hat = hat * lax.rsqrt(jnp.mean(hat * hat, axis=0, keepdims=True) + EPS)
        az = az_ref[:, hcols(hh)]
        out_ref[:, hcols(hh)] = (hat.T * (gd_ref[:, hcols(hh)] * (1.0 - lam_init))
                                 * (az * _sigmoid(az))).astype(out_ref.dtype)


def _prompt_attn(slab, fslab, g_diff, lam_params, *, seq, tq, lam_init):
    nq = seq // tq
    assert nq * tq == seq
    hp = HEADS_PER_STEP
    width = hp * A_DV
    blk = PIECE // width
    return pl.pallas_call(
        functools.partial(_prompt_attn_kernel, tq=tq, lam_init=lam_init),
        grid=(A_H // hp, nq),
        in_specs=[
            pl.BlockSpec((tq, width), lambda h, i: (i, _P_AQ * blk + h)),
            pl.BlockSpec((seq, width), lambda h, i: (0, _P_AK * blk + h), pipeline_mode=pl.Buffered(1)),
            pl.BlockSpec((seq, width), lambda h, i: (0, _P_AV * blk + h), pipeline_mode=pl.Buffered(1)),
            pl.BlockSpec((tq, width), lambda h, i: (i, (_P_AZ - N_SLAB) * blk + h)),
            pl.BlockSpec((1, width), lambda h, i: (0, h)),
            pl.BlockSpec((4, A_DK), lambda h, i: (0, 0)),
        ],
        out_specs=pl.BlockSpec((tq, width), lambda h, i: (i, h)),
        out_shape=jax.ShapeDtypeStruct((seq, A_W), BF16),
        scratch_shapes=[pltpu.VMEM((hp, nq, A_DV + ONES_ROWS, tq), BF16),
                        pltpu.VMEM((hp, tq, 2 * tq), F32), pltpu.VMEM((hp, tq, 2 * tq), F32),
                        pltpu.VMEM((hp, 1, 2 * tq), F32), pltpu.VMEM((hp, A_DV + ONES_ROWS, 2 * tq), F32)],
        compiler_params=pltpu.CompilerParams(
            dimension_semantics=("arbitrary", "arbitrary"), vmem_limit_bytes=V7X_VMEM_LIMIT),
        name="prompt_attn",
    )(slab, slab, slab, fslab, g_diff, lam_params)


def _sample_attn_kernel(pt_ref, q_ref, kn_ref, vn_ref, az_ref, *rest, ts, seqs, group, lam_init):
    del pt_ref
    n_pg = seqs * group
    ck_refs, cv_refs = rest[:n_pg], rest[n_pg:2 * n_pg]
    gd_ref, lam_ref, out_ref, qs_s, m_s, l_s, acc_s = rest[2 * n_pg:]
    p = pl.program_id(1)
    n2 = 2 * ts

    @pl.when(p == 0)
    def _():
        zero = jnp.zeros((n2, A_DV), F32)
        for sq in range(seqs):
            q = q_ref[sq * ts:(sq + 1) * ts, :]
            qbd = jnp.concatenate(
                [jnp.concatenate([_stack_maps(q[:, h * A_DV:(h + 1) * A_DV]) if hc == h else zero
                                  for hc in range(A_H)], axis=1) for h in range(A_H)], axis=0)
            qs_s[sq] = qbd.T.astype(BF16)
        m_s[...] = jnp.full_like(m_s, -jnp.inf)
        l_s[...] = jnp.zeros_like(l_s)
        acc_s[...] = jnp.zeros_like(acc_s)

    def attend(sq, k_tiles, v_tiles, mask=None):
        st = sum(_dot(jnp.concatenate(k_tiles[h], axis=0), qs_s[sq, h * A_DV:(h + 1) * A_DV, :])
                 for h in range(A_H))
        s = st.T
        if mask is not None:
            s = jnp.where(mask, s, -jnp.inf)

        def pv(pb):
            return jnp.concatenate(
                [sum(_dot(pb[h * n2:(h + 1) * n2, g * PAGE:(g + 1) * PAGE], vt) for g, vt in enumerate(v_tiles[h]))
                 for h in range(A_H)], axis=0)

        _online_softmax_step(s, pv, m_s.at[sq], l_s.at[sq], acc_s.at[sq])

    head_rows = lambda ref, h: ref[pl.ds(h, PAGE, stride=A_H), :].astype(BF16)
    for sq in range(seqs):
        mine = slice(sq * group, (sq + 1) * group)
        attend(sq, [[head_rows(r, h) for r in ck_refs[mine]] for h in range(A_H)],
               [[head_rows(r, h) for r in cv_refs[mine]] for h in range(A_H)])

    @pl.when(p == pl.num_programs(1) - 1)
    def _():
        lam = _lambda(lam_ref, lam_init)
        gd = gd_ref[...]
        pad = jnp.zeros((PAGE - ts, A_DV), F32)
        head_new = lambda a, h: jnp.concatenate([a[:, h * A_DV:(h + 1) * A_DV], pad], axis=0).astype(BF16)
        r = lax.broadcasted_iota(jnp.int32, (A_H * n2, PAGE), 0)
        cc = lax.broadcasted_iota(jnp.int32, (A_H * n2, PAGE), 1)
        for sq in range(seqs):
            rows = slice(sq * ts, (sq + 1) * ts)
            kn = kn_ref[rows, :]
            vn = vn_ref[rows, :]
            az = az_ref[rows, :]
            attend(sq, [[head_new(kn, h)] for h in range(A_H)], [[head_new(vn, h)] for h in range(A_H)],
                   mask=(r % ts) >= cc)
            o = acc_s[sq] / l_s[sq]
            for h in range(A_H):
                cols = slice(h * A_DV, (h + 1) * A_DV)
                out_ref[rows, cols] = _diff_finish(o[h * n2:(h + 1) * n2, :], ts, lam, gd[:, cols], az[:, cols], lam_init)


def _sample_attn(page_table, slab, k_new, v_new, fslab, cache_k, cache_v, g_diff, lam_params, *, batch, ts, lam_init):
    n_pages = page_table.shape[1]
    pool = cache_k.shape[0]
    seqs = math.gcd(batch, SAMPLE_SEQS_PER_STEP)
    group = math.gcd(n_pages, SAMPLE_PAGES_PER_STEP // seqs)
    ck = cache_k.reshape(pool, PAGE * A_H, A_DV)
    cv = cache_v.reshape(pool, PAGE * A_H, A_DV)
    rows = A_H * 2 * ts
    tok = lambda col: pl.BlockSpec((seqs * ts, PIECE), lambda b, p, pt: (b, col))
    page = lambda sq, g: pl.BlockSpec((None, PAGE * A_H, A_DV),
                                      lambda b, p, pt: (pt[b * seqs + sq, p * group + g], 0, 0))
    pages = [page(sq, g) for sq in range(seqs) for g in range(group)]
    grid_spec = pltpu.PrefetchScalarGridSpec(
        num_scalar_prefetch=1,
        grid=(batch // seqs, n_pages // group),
        in_specs=[tok(_P_AQ), tok(0), tok(0), tok(_P_AZ - N_SLAB)] + pages + pages + [
            pl.BlockSpec((1, A_W), lambda b, p, pt: (0, 0)),
            pl.BlockSpec((4, A_DK), lambda b, p, pt: (0, 0)),
        ],
        out_specs=pl.BlockSpec((seqs * ts, A_W), lambda b, p, pt: (b, 0)),
        scratch_shapes=[pltpu.VMEM((seqs, A_W, rows), BF16)] + [pltpu.VMEM((seqs, rows, A_DV), F32)] * 3,
    )
    n_pg = seqs * group
    return pl.pallas_call(
        functools.partial(_sample_attn_kernel, ts=ts, seqs=seqs, group=group, lam_init=lam_init),
        grid_spec=grid_spec,
        out_shape=jax.ShapeDtypeStruct((batch * ts, A_W), F32),
        compiler_params=pltpu.CompilerParams(
            dimension_semantics=("arbitrary", "arbitrary"), vmem_limit_bytes=V7X_VMEM_LIMIT),
        name="sample_attn",
    )(page_table, slab, k_new, v_new, fslab, *([ck] * n_pg), *([cv] * n_pg), g_diff, lam_params)


def _outproj_kernel(mm_ref, ma_ref, x_ref, w_ref, g_ref, out_ref):
    y = (_dot(mm_ref[...].astype(BF16), w_ref[0:M_W, :])
         + _dot(ma_ref[...].astype(BF16), w_ref[M_W:M_W + A_W, :]))
    out_ref[...] = x_ref[...] + y * lax.rsqrt(jnp.mean(y * y, axis=-1, keepdims=True) + EPS) * g_ref[...]


def _outproj(mixed_m, mixed_a, x, w_out, g_post, *, tm):
    rows = x.shape[0]
    assert rows % tm == 0
    return pl.pallas_call(
        _outproj_kernel,
        grid=(rows // tm,),
        in_specs=[
            pl.BlockSpec((tm, M_W), lambda i: (i, 0)),
            pl.BlockSpec((tm, A_W), lambda i: (i, 0)),
            pl.BlockSpec((tm, D_MODEL), lambda i: (i, 0)),
            pl.BlockSpec((M_W + A_W, D_MODEL), lambda i: (0, 0)),
            pl.BlockSpec((1, D_MODEL), lambda i: (0, 0)),
        ],
        out_specs=pl.BlockSpec((tm, D_MODEL), lambda i: (i, 0)),
        out_shape=jax.ShapeDtypeStruct((rows, D_MODEL), F32),
        compiler_params=pltpu.CompilerParams(
            dimension_semantics=("arbitrary",), vmem_limit_bytes=V7X_VMEM_LIMIT),
        name="outproj",
    )(mixed_m, mixed_a, x, w_out, g_post)


def _rope_tables(pos):
    half = A_DK // 2
    inv = ROPE_THETA ** (-jnp.arange(half, dtype=F32) * 2.0 / A_DK)
    ang = pos.astype(F32)[:, None] * inv[None, :]
    cos, sin = jnp.cos(ang), jnp.sin(ang)
    reps = A_DV // A_DK
    return (jnp.tile(jnp.concatenate([cos, cos], axis=1), (1, reps)),
            jnp.tile(jnp.concatenate([-sin, sin], axis=1), (1, reps)))


def _row_tile(rows, want):
    return math.gcd(rows, want)


def _layer(x, pos, state, attend, weights, l, *, batch, seq, chunk, slab_dtype, mix_dtype):
    (w_main, wg, bias_row, bias_col, w_out, g_pre, g_post, g_mlstm) = weights
    rows = batch * seq
    x2 = x.reshape(rows, D_MODEL)
    cos, sin = _rope_tables(pos)
    slab, k_rot, v_new, fslab, gates, gatest = _inproj(
        x2, g_pre, w_main, wg, cos, sin, tm=_row_tile(rows, 512), slab_dtype=slab_dtype)
    nc = seq // chunk
    gatest3 = gatest.reshape(8, batch * nc, chunk).transpose(1, 0, 2)
    mixed_m, c_new, n_new, m_new = _mlstm(
        slab, fslab, gates, gatest3, bias_row, bias_col, g_mlstm, state,
        batch=batch, seq=seq, L=chunk, out_dtype=mix_dtype)
    mixed_a = attend(slab, k_rot, v_new, fslab)
    y = _outproj(mixed_m, mixed_a, x2, w_out, g_post, tm=_row_tile(rows, 512))
    return (y.reshape(batch, seq, D_MODEL),
            k_rot.reshape(batch, seq, A_H, 2 * A_DK), v_new.reshape(batch, seq, A_H, A_DV),
            c_new, n_new.reshape(batch, M_H, M_D), m_new.reshape(batch, M_H))


def kernel(x_prompt, x_sample, cache_k, cache_v, state_C, state_n, state_m, page_table, w_in, b_i, b_f, w_out, g_pre, g_post, g_mlstm, g_diff, lam_q1, lam_k1, lam_q2, lam_k2):
    depth = w_in.shape[0]
    B, T, _ = x_prompt.shape
    DB, TS, _ = x_sample.shape
    n_past = page_table.shape[1] * PAGE
    pos_p = jnp.arange(T)
    pos_s = jnp.tile(n_past + jnp.arange(TS), DB)
    assert B == 1

    xp, xs = x_prompt, x_sample
    outs = [[] for _ in range(10)]
    for l in range(depth):
        lam_init = 0.8 - 0.6 * math.exp(-0.3 * l)
        wt = w_in[l].T
        w_main = _repack_w_in(wt)
        wg = _gate_cols(wt)
        bias8 = jnp.concatenate([b_i[l], b_f[l]])
        bias_row = jnp.pad(bias8, (0, GATE_PAD - N_GATES)).reshape(1, GATE_PAD)
        bias_col = bias8.reshape(N_GATES, 1)
        weights = (w_main, wg, bias_row, bias_col, w_out[l].astype(BF16),
                   g_pre[l].reshape(1, D_MODEL), g_post[l].reshape(1, D_MODEL), g_mlstm[l].reshape(1, M_W))
        gd = g_diff[l].reshape(1, A_W)
        lam_params = jnp.stack([lam_q1[l], lam_k1[l], lam_q2[l], lam_k2[l]])

        def prompt_attend(slab, k_rot, v_new, fslab):
            return _prompt_attn(slab, fslab, gd, lam_params, seq=T, tq=_row_tile(T, 512), lam_init=lam_init)

        def sample_attend(slab, k_rot, v_new, fslab):
            return _sample_attn(page_table, slab, k_rot, v_new, fslab, cache_k[l], cache_v[l], gd, lam_params,
                                batch=DB, ts=TS, lam_init=lam_init)

        xp, kk, vv, C, n, m = _layer(xp, pos_p, None, prompt_attend, weights, l,
                                     batch=B, seq=T, chunk=math.gcd(T, 256), slab_dtype=BF16, mix_dtype=BF16)
        for lst, a in zip(outs[:5], (kk, vv, C, n, m)):
            lst.append(a)
        state = (state_C[l], state_n[l].reshape(DB, M_H, 1, M_D), state_m[l].reshape(DB, M_H, 1, 1))
        xs, kk, vv, C, n, m = _layer(xs, pos_s, state, sample_attend, weights, l,
                                     batch=DB, seq=TS, chunk=TS, slab_dtype=F32, mix_dtype=F32)
        for lst, a in zip(outs[5:], (kk, vv, C, n, m)):
            lst.append(a)

    return (xp, xs) + tuple(jnp.stack(o) for o in outs)
```

```python
import functools
import math

import jax
import jax.numpy as jnp
from jax import lax
from jax.experimental import pallas as pl
from jax.experimental.pallas import tpu as pltpu

F32 = jnp.float32
BF16 = jnp.bfloat16

D_MODEL = 2048
M_H = 4
M_D = 256
M_W = M_H * M_D
A_H = 8
A_DV = 128
A_DK = 64
A_W = A_H * A_DV
PAGE = 128
ROPE_THETA = 10000.0
EPS = 1e-6
PIECE = 1024
LANES = 128
MXU_N = 256
N_GATES = 2 * M_H
REPACK_W = 512
GATE_PAD = LANES
HEADS_PER_STEP = 4
SAMPLE_SEQS_PER_STEP = 2
SAMPLE_PAGES_PER_STEP = 16
ONES_ROWS = 16
V7X_VMEM_LIMIT = 56 * 1024 * 1024
Q_SCALE = A_DK ** -0.5 * math.log2(math.e)

_P_MQ, _P_MK, _P_MV, _P_AQ, _P_AK, _P_AV, _P_MO, _P_MZ, _P_AZ = range(9)
N_PIECES = 9
N_SLAB = 6
N_FSLAB = 3


def _dot(a, b):
    return jnp.dot(a, b, preferred_element_type=F32)


def _dot_nt(a, b):
    return lax.dot_general(a, b, (((1,), (1,)), ((), ())), preferred_element_type=F32)


def _dot_tn(a, b):
    return lax.dot_general(a, b, (((0,), (0,)), ((), ())), preferred_element_type=F32)


def _split2(a):
    hi = a.astype(BF16)
    lo = (a - hi.astype(F32)).astype(BF16)
    return hi, lo


def _split3(a):
    a1 = a.astype(BF16)
    r1 = a - a1.astype(F32)
    a2 = r1.astype(BF16)
    a3 = (r1 - a2.astype(F32)).astype(BF16)
    return a1, a2, a3


def _sigmoid(x):
    return 1.0 / (1.0 + jnp.exp(-x))


def _log_sigmoid(x):
    return jnp.minimum(x, 0.0) - jnp.log(1.0 + jnp.exp(-jnp.abs(x)))


def _repack_kernel(row_off, wt_ref, out_ref):
    del row_off
    out_ref[...] = wt_ref[...].T.astype(BF16)


def _repack_w_in(wt):
    gate_row = 5 * M_W
    a_row = gate_row + N_GATES
    src = {_P_MQ: 0, _P_MK: M_W, _P_MV: 2 * M_W, _P_MO: 3 * M_W, _P_MZ: 4 * M_W,
           _P_AQ: a_row, _P_AK: a_row + A_W, _P_AV: a_row + 2 * A_W, _P_AZ: a_row + 3 * A_W}
    offs = [src[piece] + k * REPACK_W for piece in range(N_PIECES) for k in range(PIECE // REPACK_W)]
    assert all(o % 8 == 0 for o in offs)
    grid_spec = pltpu.PrefetchScalarGridSpec(
        num_scalar_prefetch=1,
        grid=(len(offs),),
        in_specs=[pl.BlockSpec((pl.Element(REPACK_W), pl.Element(D_MODEL)),
                               lambda j, off: (pl.multiple_of(off[j], 8), 0))],
        out_specs=pl.BlockSpec((D_MODEL, REPACK_W), lambda j, off: (0, j)),
    )
    return pl.pallas_call(
        _repack_kernel,
        grid_spec=grid_spec,
        out_shape=jax.ShapeDtypeStruct((D_MODEL, N_PIECES * PIECE), BF16),
        compiler_params=pltpu.CompilerParams(dimension_semantics=("arbitrary",), vmem_limit_bytes=V7X_VMEM_LIMIT),
        name="repack_w_in",
    )(jnp.asarray(offs, jnp.int32), wt)


def _gate_cols_kernel(wt_ref, out_ref):
    lane = lax.broadcasted_iota(jnp.int32, out_ref.shape, 1)
    out_ref[...] = jnp.where(lane < N_GATES, wt_ref[...].T, 0.0)


def _gate_cols(wt):
    return pl.pallas_call(
        _gate_cols_kernel,
        grid=(1,),
        in_specs=[pl.BlockSpec((pl.Element(GATE_PAD), pl.Element(D_MODEL)), lambda i: (5 * M_W, 0))],
        out_specs=pl.BlockSpec((D_MODEL, GATE_PAD), lambda i: (0, 0)),
        out_shape=jax.ShapeDtypeStruct((D_MODEL, GATE_PAD), F32),
        name="gate_cols",
    )(wt)


def _inproj_kernel(x_ref, gpre_ref, w_ref, wg_ref, cos_ref, sin_ref,
                   slab_ref, k_ref, v_ref, fslab_ref, gates_ref, gatest_ref, h_ref):
    j = pl.program_id(1)
    tm = x_ref.shape[0]

    @pl.when(j == 0)
    def _():
        x = x_ref[...]
        h = x * lax.rsqrt(jnp.mean(x * x, axis=-1, keepdims=True) + EPS) * gpre_ref[...]
        h_hi, h_lo = _split2(h)
        h_ref[...] = h_hi
        wg_hi, wg_lo = _split2(wg_ref[...])
        hi_pass = _dot(h_hi, jnp.concatenate([wg_hi, wg_lo], axis=1))
        gates = _dot(h_lo, wg_hi) + hi_pass[:, GATE_PAD:] + hi_pass[:, :GATE_PAD]
        gates_ref[...] = gates
        gatest_ref[...] = gates.T[0:8, :]

    def matmul_chunks():
        for c in range(PIECE // MXU_N):
            cols = slice(c * MXU_N, (c + 1) * MXU_N)
            yield cols, _dot(h_ref[...], w_ref[:, cols])

    def rope_chunks():
        cos = cos_ref[...]
        sin = sin_ref[...]
        lane = lax.broadcasted_iota(jnp.int32, (tm, A_DV), 1)
        first_half = (lane % A_DK) < (A_DK // 2)
        for cols, acc in matmul_chunks():
            for c in range(MXU_N // A_DV):
                xc = acc[:, c * A_DV:(c + 1) * A_DV]
                rot = jnp.where(first_half, pltpu.roll(xc, A_DV - A_DK // 2, 1), pltpu.roll(xc, A_DK // 2, 1))
                yield slice(cols.start + c * A_DV, cols.start + (c + 1) * A_DV), xc * cos + rot * sin

    @pl.when((j == _P_MQ) | (j == _P_MV))
    def _():
        for cols, acc in matmul_chunks():
            slab_ref[:, cols] = acc.astype(slab_ref.dtype)

    @pl.when(j == _P_MK)
    def _():
        for cols, acc in matmul_chunks():
            slab_ref[:, cols] = (acc * (M_D ** -0.5)).astype(slab_ref.dtype)

    @pl.when(j == _P_AQ)
    def _():
        for cols, y in rope_chunks():
            slab_ref[:, cols] = (y * Q_SCALE).astype(slab_ref.dtype)

    @pl.when(j == _P_AK)
    def _():
        for cols, y in rope_chunks():
            k_ref[:, cols] = y
            slab_ref[:, cols] = y.astype(slab_ref.dtype)

    @pl.when(j == _P_AV)
    def _():
        for cols, acc in matmul_chunks():
            v_ref[:, cols] = acc
            slab_ref[:, cols] = acc.astype(slab_ref.dtype)

    @pl.when(j >= N_SLAB)
    def _():
        for cols, acc in matmul_chunks():
            fslab_ref[:, cols] = acc


def _inproj(x, g_pre, w_main, wg, cos, sin, *, tm, slab_dtype):
    rows = x.shape[0]
    assert rows % tm == 0
    grid = (rows // tm, N_PIECES)
    row_blk = lambda width: pl.BlockSpec((tm, width), lambda i, j: (i, 0))
    return pl.pallas_call(
        _inproj_kernel,
        grid=grid,
        in_specs=[
            row_blk(D_MODEL),
            pl.BlockSpec((1, D_MODEL), lambda i, j: (0, 0)),
            pl.BlockSpec((D_MODEL, PIECE), lambda i, j: (0, j)),
            pl.BlockSpec((D_MODEL, GATE_PAD), lambda i, j: (0, 0)),
            row_blk(A_DV),
            row_blk(A_DV),
        ],
        out_specs=[
            pl.BlockSpec((tm, PIECE), lambda i, j: (i, jnp.minimum(j, N_SLAB - 1))),
            row_blk(PIECE),
            row_blk(PIECE),
            pl.BlockSpec((tm, PIECE), lambda i, j: (i, jnp.maximum(j - N_SLAB, 0))),
            row_blk(GATE_PAD),
            pl.BlockSpec((8, tm), lambda i, j: (0, i)),
        ],
        out_shape=[
            jax.ShapeDtypeStruct((rows, N_SLAB * PIECE), slab_dtype),
            jax.ShapeDtypeStruct((rows, PIECE), F32),
            jax.ShapeDtypeStruct((rows, PIECE), F32),
            jax.ShapeDtypeStruct((rows, N_FSLAB * PIECE), F32),
            jax.ShapeDtypeStruct((rows, GATE_PAD), F32),
            jax.ShapeDtypeStruct((8, rows), F32),
        ],
        scratch_shapes=[pltpu.VMEM((tm, D_MODEL), BF16)],
        compiler_params=pltpu.CompilerParams(
            dimension_semantics=("arbitrary", "arbitrary"), vmem_limit_bytes=V7X_VMEM_LIMIT),
        name="inproj",
    )(x, g_pre, w_main, wg, cos, sin)


def _mlstm_kernel(*refs, L, has_state):
    if has_state:
        (q_ref, k_ref, v_ref, mo_ref, mz_ref, g_ref, gt_ref, bias_ref, biast_ref, gm_ref,
         c0_ref, n0_ref, m0_ref, out_ref, cout_ref, nout_ref, mout_ref, c_s, n_s, m_s) = refs
    else:
        (q_ref, k_ref, v_ref, mo_ref, mz_ref, g_ref, gt_ref, bias_ref, biast_ref, gm_ref,
         out_ref, cout_ref, nout_ref, mout_ref, c_s, n_s, m_s) = refs
    c = pl.program_id(1)

    @pl.when(c == 0)
    def _():
        if has_state:
            c_s[...] = c0_ref[...]
            n_s[...] = n0_ref[...]
            m_s[...] = m0_ref[...]
        else:
            c_s[...] = jnp.zeros_like(c_s)
            n_s[...] = jnp.zeros_like(n_s)
            m_s[...] = jnp.zeros_like(m_s)

    gcol = g_ref[...] + bias_ref[...]
    grow = gt_ref[...] + biast_ref[...]
    r_i = lax.broadcasted_iota(jnp.int32, (L, L), 0)
    c_i = lax.broadcasted_iota(jnp.int32, (L, L), 1)
    causal = r_i >= c_i
    lf_col = _log_sigmoid(gcol)
    lf_row = _log_sigmoid(grow)
    if L >= 128:
        tri = jnp.where(causal, 1.0, 0.0).astype(BF16)
        trit = jnp.where(r_i <= c_i, 1.0, 0.0).astype(BF16)
        a1, a2, a3 = _split3(lf_col)
        bcols = _dot(tri, a3) + _dot(tri, a2) + _dot(tri, a1)
        a1, a2, a3 = _split3(lf_row)
        brows = _dot(a3, trit) + _dot(a2, trit) + _dot(a1, trit)
    else:
        rows = [lf_col[0:1, :]]
        for t in range(1, L):
            rows.append(rows[-1] + lf_col[t:t + 1, :])
        bcols = jnp.concatenate(rows, axis=0)
        lane_t = lax.broadcasted_iota(jnp.int32, (8, L), 1)
        brows = jnp.zeros((8, L), F32)
        for s in range(L):
            brows = brows + jnp.where(lane_t >= s, lf_row[:, s:s + 1], 0.0)
    for hd in range(M_H):
        cols = slice(hd * M_D, (hd + 1) * M_D)
        q = q_ref[:, cols].astype(BF16)
        k = k_ref[:, cols].astype(BF16)
        v = v_ref[:, cols].astype(BF16)
        i_col = gcol[:, hd:hd + 1]
        b_col = bcols[:, M_H + hd:M_H + hd + 1]
        i_row = grow[hd:hd + 1, :]
        b_row = brows[M_H + hd:M_H + hd + 1, :]

        m_prev = m_s[hd]
        logd = jnp.where(causal, b_col - b_row + i_row, -jnp.inf)
        m_col = jnp.maximum(b_col + m_prev, jnp.max(logd, axis=1, keepdims=True))
        dmat = jnp.exp(logd - m_col)
        inter = jnp.exp(b_col + m_prev - m_col)

        cmat = c_s[hd]
        nvec = n_s[hd]
        w = _dot_nt(q, k) * dmat
        num = _dot(w.astype(BF16), v) + inter * _dot_nt(q, cmat.astype(BF16))
        qf = q.astype(F32)
        den = jnp.sum(w, axis=1, keepdims=True) + inter * jnp.sum(qf * nvec, axis=1, keepdims=True)
        hstate = num / jnp.maximum(jnp.abs(den), jnp.exp(-m_col))

        m_new = m_col[L - 1:L, :]
        b_last = b_col[L - 1:L, :]
        gdec = jnp.exp(b_last - b_col + i_col - m_new)
        decay = jnp.exp(b_last + m_prev - m_new)
        gk = gdec * k.astype(F32)
        c_s[hd] = decay * cmat + _dot_tn(v, gk.astype(BF16))
        n_s[hd] = decay * nvec + jnp.sum(gk, axis=0, keepdims=True)
        m_s[hd] = m_new

        hm = _sigmoid(mo_ref[:, cols]) * hstate
        hm = hm * lax.rsqrt(jnp.mean(hm * hm, axis=-1, keepdims=True) + EPS) * gm_ref[:, cols]
        mz = mz_ref[:, cols]
        out_ref[:, cols] = (hm * (mz * _sigmoid(mz))).astype(out_ref.dtype)

    @pl.when(c == pl.num_programs(1) - 1)
    def _():
        cout_ref[...] = c_s[...]
        nout_ref[...] = n_s[...]
        mout_ref[...] = m_s[...]


def _mlstm(slab, fslab, gates, gatest3, bias_row, bias_col, g_mlstm, state, *, batch, seq, L, out_dtype):
    nc = seq // L
    assert nc * L == seq
    has_state = state is not None
    tok = lambda piece: pl.BlockSpec((L, M_W), lambda b, c: (b * nc + c, piece))
    const = lambda shape: pl.BlockSpec(shape, lambda b, c: (0,) * len(shape))
    per_b = lambda r, cdim: pl.BlockSpec((None, M_H, r, cdim), lambda b, c: (b, 0, 0, 0))
    in_specs = [
        tok(_P_MQ), tok(_P_MK), tok(_P_MV),
        tok(_P_MO - N_SLAB), tok(_P_MZ - N_SLAB),
        pl.BlockSpec((L, GATE_PAD), lambda b, c: (b * nc + c, 0)),
        pl.BlockSpec((None, 8, L), lambda b, c: (b * nc + c, 0, 0)),
        const((1, GATE_PAD)), const((8, 1)), const((1, M_W)),
    ]
    args = [slab, slab, slab, fslab, fslab, gates, gatest3, bias_row, bias_col, g_mlstm]
    if has_state:
        in_specs += [per_b(M_D, M_D), per_b(1, M_D), per_b(1, 1)]
        args += list(state)
    return pl.pallas_call(
        functools.partial(_mlstm_kernel, L=L, has_state=has_state),
        grid=(batch, nc),
        in_specs=in_specs,
        out_specs=[
            pl.BlockSpec((L, M_W), lambda b, c: (b * nc + c, 0)),
            per_b(M_D, M_D), per_b(1, M_D), per_b(1, 1),
        ],
        out_shape=[
            jax.ShapeDtypeStruct((batch * seq, M_W), out_dtype),
            jax.ShapeDtypeStruct((batch, M_H, M_D, M_D), F32),
            jax.ShapeDtypeStruct((batch, M_H, 1, M_D), F32),
            jax.ShapeDtypeStruct((batch, M_H, 1, 1), F32),
        ],
        scratch_shapes=[pltpu.VMEM((M_H, M_D, M_D), F32), pltpu.VMEM((M_H, 1, M_D), F32), pltpu.VMEM((M_H, 1, 1), F32)],
        compiler_params=pltpu.CompilerParams(
            dimension_semantics=("arbitrary", "arbitrary"), vmem_limit_bytes=V7X_VMEM_LIMIT),
        name="mlstm_state" if has_state else "mlstm_fresh",
    )(*args)


def _lambda(lam_ref, lam_init):
    lam = lam_ref[...]
    s1 = jnp.sum(lam[0:1, :] * lam[1:2, :], axis=1, keepdims=True)
    s2 = jnp.sum(lam[2:3, :] * lam[3:4, :], axis=1, keepdims=True)
    return jnp.exp(s1) - jnp.exp(s2) + lam_init


def _stack_maps(qh):
    lane = lax.broadcasted_iota(jnp.int32, qh.shape, 1)
    zero = jnp.zeros_like(qh)
    return jnp.concatenate([jnp.where(lane < A_DK, qh, zero), jnp.where(lane >= A_DK, qh, zero)], axis=0)


def _online_softmax_step(s, pv_fn, m_s, l_s, acc_s):
    reps = s.shape[1] // A_DV
    m_prev = m_s[...]
    m_next = jnp.maximum(m_prev, jnp.max(s, axis=1, keepdims=True))
    alpha = jnp.exp2(m_prev - m_next)
    p = jnp.exp2(s - (jnp.tile(m_next, (1, reps)) if reps > 1 else m_next))
    l_s[...] = alpha * l_s[...] + jnp.sum(p, axis=1, keepdims=True)
    acc_s[...] = alpha * acc_s[...] + pv_fn(p.astype(BF16))
    m_s[...] = m_next


def _diff_finish(o, n, lam, gd, az, lam_init):
    ha = o[:n, :] - lam * o[n:, :]
    ha = ha * lax.rsqrt(jnp.mean(ha * ha, axis=-1, keepdims=True) + EPS) * gd * (1.0 - lam_init)
    return ha * (az * _sigmoid(az))


def _prompt_attn_kernel(q_ref, k_ref, v_ref, az_ref, gd_ref, lam_ref, out_ref, vt_s, sa_s, sb_s, m_s, acc_s, *, tq, lam_init):
    qi = pl.program_id(1)
    n_tiles = k_ref.shape[0] // tq
    heads = range(HEADS_PER_STEP)
    hcols = lambda hh: slice(hh * A_DV, (hh + 1) * A_DV)

    @pl.when(qi == 0)
    def _():
        def xpose(c, carry):
            start = pl.multiple_of(c * tq, tq)
            for hh in heads:
                vt = v_ref[pl.ds(start, tq), hcols(hh)].astype(F32).T.astype(BF16)
                vt_s[hh, c] = jnp.concatenate([vt, jnp.ones((ONES_ROWS, tq), BF16)], axis=0)
            return carry
        lax.fori_loop(0, n_tiles, xpose, 0)

    qs = [_stack_maps(q_ref[:, hcols(hh)]) for hh in heads]
    m_s[...] = jnp.full_like(m_s, -jnp.inf)
    acc_s[...] = jnp.zeros_like(acc_s)

    def scores(ki, dst):
        start = pl.multiple_of(ki * tq, tq)
        for hh in heads:
            dst[hh] = _dot_nt(k_ref[pl.ds(start, tq), hcols(hh)], qs[hh])

    def consume(src, ki, masked):
        for hh in heads:
            st = src[hh]
            if masked:
                key = lax.broadcasted_iota(jnp.int32, (tq, 2 * tq), 0)
                qry = lax.broadcasted_iota(jnp.int32, (tq, 2 * tq), 1)
                qry = jnp.where(qry >= tq, qry - tq, qry)
                st = jnp.where(qry >= key, st, -jnp.inf)
            m_prev = m_s[hh]
            m_next = jnp.maximum(m_prev, jnp.max(st, axis=0, keepdims=True))
            alpha = jnp.exp2(m_prev - m_next)
            pt = jnp.exp2(st - m_next).astype(BF16)
            acc_s[hh] = alpha * acc_s[hh] + _dot(vt_s[hh, ki], pt)
            m_s[hh] = m_next

    scores(0, sa_s)

    def pair(j, carry):
        scores(2 * j + 1, sb_s)
        consume(sa_s, 2 * j, False)
        scores(2 * j + 2, sa_s)
        consume(sb_s, 2 * j + 1, False)
        return carry

    lax.fori_loop(0, qi // 2, pair, 0)

    @pl.when(qi % 2 == 0)
    def _():
        consume(sa_s, qi, True)

    @pl.when(qi % 2 == 1)
    def _():
        scores(qi, sb_s)
        consume(sa_s, qi - 1, False)
        consume(sb_s, qi, True)

    lam = _lambda(lam_ref, lam_init)
    for hh in heads:
        acc = acc_s[hh]
        ot = acc[:A_DV, :] / acc[A_DV:A_DV + 1, :]
        hat = ot[:, :tq] - lam * ot[:, tq:]
        hat = hat * lax.rsqrt(jnp.mean(hat * hat, axis=0, keepdims=True) + EPS)
        az = az_ref[:, hcols(hh)]
        out_ref[:, hcols(hh)] = (hat.T * (gd_ref[:, hcols(hh)] * (1.0 - lam_init))
                                 * (az * _sigmoid(az))).astype(out_ref.dtype)


def _prompt_attn(slab, fslab, g_diff, lam_params, *, seq, tq, lam_init):
    nq = seq // tq
    assert nq * tq == seq
    hp = HEADS_PER_STEP
    width = hp * A_DV
    blk = PIECE // width
    return pl.pallas_call(
        functools.partial(_prompt_attn_kernel, tq=tq, lam_init=lam_init),
        grid=(A_H // hp, nq),
        in_specs=[
            pl.BlockSpec((tq, width), lambda h, i: (i, _P_AQ * blk + h)),
            pl.BlockSpec((seq, width), lambda h, i: (0, _P_AK * blk + h), pipeline_mode=pl.Buffered(1)),
            pl.BlockSpec((seq, width), lambda h, i: (0, _P_AV * blk + h), pipeline_mode=pl.Buffered(1)),
            pl.BlockSpec((tq, width), lambda h, i: (i, (_P_AZ - N_SLAB) * blk + h)),
            pl.BlockSpec((1, width), lambda h, i: (0, h)),
            pl.BlockSpec((4, A_DK), lambda h, i: (0, 0)),
        ],
        out_specs=pl.BlockSpec((tq, width), lambda h, i: (i, h)),
        out_shape=jax.ShapeDtypeStruct((seq, A_W), BF16),
        scratch_shapes=[pltpu.VMEM((hp, nq, A_DV + ONES_ROWS, tq), BF16),
                        pltpu.VMEM((hp, tq, 2 * tq), F32), pltpu.VMEM((hp, tq, 2 * tq), F32),
                        pltpu.VMEM((hp, 1, 2 * tq), F32), pltpu.VMEM((hp, A_DV + ONES_ROWS, 2 * tq), F32)],
        compiler_params=pltpu.CompilerParams(
            dimension_semantics=("arbitrary", "arbitrary"), vmem_limit_bytes=V7X_VMEM_LIMIT),
        name="prompt_attn",
    )(slab, slab, slab, fslab, g_diff, lam_params)


def _sample_attn_kernel(pt_ref, q_ref, kn_ref, vn_ref, az_ref, *rest, ts, seqs, group, lam_init):
    del pt_ref
    n_pg = seqs * group
    ck_refs, cv_refs = rest[:n_pg], rest[n_pg:2 * n_pg]
    gd_ref, lam_ref, out_ref, qs_s, m_s, l_s, acc_s = rest[2 * n_pg:]
    p = pl.program_id(1)
    n2 = 2 * ts

    @pl.when(p == 0)
    def _():
        zero = jnp.zeros((n2, A_DV), F32)
        for sq in range(seqs):
            q = q_ref[sq * ts:(sq + 1) * ts, :]
            qbd = jnp.concatenate(
                [jnp.concatenate([_stack_maps(q[:, h * A_DV:(h + 1) * A_DV]) if hc == h else zero
                                  for hc in range(A_H)], axis=1) for h in range(A_H)], axis=0)
            qs_s[sq] = qbd.T.astype(BF16)
        m_s[...] = jnp.full_like(m_s, -jnp.inf)
        l_s[...] = jnp.zeros_like(l_s)
        acc_s[...] = jnp.zeros_like(acc_s)

    def attend(sq, k_tiles, v_tiles, mask=None):
        st = sum(_dot(jnp.concatenate(k_tiles[h], axis=0), qs_s[sq, h * A_DV:(h + 1) * A_DV, :])
                 for h in range(A_H))
        s = st.T
        if mask is not None:
            s = jnp.where(mask, s, -jnp.inf)

        def pv(pb):
            return jnp.concatenate(
                [sum(_dot(pb[h * n2:(h + 1) * n2, g * PAGE:(g + 1) * PAGE], vt) for g, vt in enumerate(v_tiles[h]))
                 for h in range(A_H)], axis=0)

        _online_softmax_step(s, pv, m_s.at[sq], l_s.at[sq], acc_s.at[sq])

    head_rows = lambda ref, h: ref[pl.ds(h, PAGE, stride=A_H), :].astype(BF16)
    for sq in range(seqs):
        mine = slice(sq * group, (sq + 1) * group)
        attend(sq, [[head_rows(r, h) for r in ck_refs[mine]] for h in range(A_H)],
               [[head_rows(r, h) for r in cv_refs[mine]] for h in range(A_H)])

    @pl.when(p == pl.num_programs(1) - 1)
    def _():
        lam = _lambda(lam_ref, lam_init)
        gd = gd_ref[...]
        pad = jnp.zeros((PAGE - ts, A_DV), F32)
        head_new = lambda a, h: jnp.concatenate([a[:, h * A_DV:(h + 1) * A_DV], pad], axis=0).astype(BF16)
        r = lax.broadcasted_iota(jnp.int32, (A_H * n2, PAGE), 0)
        cc = lax.broadcasted_iota(jnp.int32, (A_H * n2, PAGE), 1)
        for sq in range(seqs):
            rows = slice(sq * ts, (sq + 1) * ts)
            kn = kn_ref[rows, :]
            vn = vn_ref[rows, :]
            az = az_ref[rows, :]
            attend(sq, [[head_new(kn, h)] for h in range(A_H)], [[head_new(vn, h)] for h in range(A_H)],
                   mask=(r % ts) >= cc)
            o = acc_s[sq] / l_s[sq]
            for h in range(A_H):
                cols = slice(h * A_DV, (h + 1) * A_DV)
                out_ref[rows, cols] = _diff_finish(o[h * n2:(h + 1) * n2, :], ts, lam, gd[:, cols], az[:, cols], lam_init)


def _sample_attn(page_table, slab, k_new, v_new, fslab, cache_k, cache_v, g_diff, lam_params, *, batch, ts, lam_init):
    n_pages = page_table.shape[1]
    pool = cache_k.shape[0]
    seqs = math.gcd(batch, SAMPLE_SEQS_PER_STEP)
    group = math.gcd(n_pages, SAMPLE_PAGES_PER_STEP // seqs)
    ck = cache_k.reshape(pool, PAGE * A_H, A_DV)
    cv = cache_v.reshape(pool, PAGE * A_H, A_DV)
    rows = A_H * 2 * ts
    tok = lambda col: pl.BlockSpec((seqs * ts, PIECE), lambda b, p, pt: (b, col))
    page = lambda sq, g: pl.BlockSpec((None, PAGE * A_H, A_DV),
                                      lambda b, p, pt: (pt[b * seqs + sq, p * group + g], 0, 0))
    pages = [page(sq, g) for sq in range(seqs) for g in range(group)]
    grid_spec = pltpu.PrefetchScalarGridSpec(
        num_scalar_prefetch=1,
        grid=(batch // seqs, n_pages // group),
        in_specs=[tok(_P_AQ), tok(0), tok(0), tok(_P_AZ - N_SLAB)] + pages + pages + [
            pl.BlockSpec((1, A_W), lambda b, p, pt: (0, 0)),
            pl.BlockSpec((4, A_DK), lambda b, p, pt: (0, 0)),
        ],
        out_specs=pl.BlockSpec((seqs * ts, A_W), lambda b, p, pt: (b, 0)),
        scratch_shapes=[pltpu.VMEM((seqs, A_W, rows), BF16)] + [pltpu.VMEM((seqs, rows, A_DV), F32)] * 3,
    )
    n_pg = seqs * group
    return pl.pallas_call(
        functools.partial(_sample_attn_kernel, ts=ts, seqs=seqs, group=group, lam_init=lam_init),
        grid_spec=grid_spec,
        out_shape=jax.ShapeDtypeStruct((batch * ts, A_W), F32),
        compiler_params=pltpu.CompilerParams(
            dimension_semantics=("arbitrary", "arbitrary"), vmem_limit_bytes=V7X_VMEM_LIMIT),
        name="sample_attn",
    )(page_table, slab, k_new, v_new, fslab, *([ck] * n_pg), *([cv] * n_pg), g_diff, lam_params)


def _outproj_kernel(mm_ref, ma_ref, x_ref, w_ref, g_ref, out_ref):
    y = (_dot(mm_ref[...].astype(BF16), w_ref[0:M_W, :])
         + _dot(ma_ref[...].astype(BF16), w_ref[M_W:M_W + A_W, :]))
    out_ref[...] = x_ref[...] + y * lax.rsqrt(jnp.mean(y * y, axis=-1, keepdims=True) + EPS) * g_ref[...]


def _outproj(mixed_m, mixed_a, x, w_out, g_post, *, tm):
    rows = x.shape[0]
    assert rows % tm == 0
    return pl.pallas_call(
        _outproj_kernel,
        grid=(rows // tm,),
        in_specs=[
            pl.BlockSpec((tm, M_W), lambda i: (i, 0)),
            pl.BlockSpec((tm, A_W), lambda i: (i, 0)),
            pl.BlockSpec((tm, D_MODEL), lambda i: (i, 0)),
            pl.BlockSpec((M_W + A_W, D_MODEL), lambda i: (0, 0)),
            pl.BlockSpec((1, D_MODEL), lambda i: (0, 0)),
        ],
        out_specs=pl.BlockSpec((tm, D_MODEL), lambda i: (i, 0)),
        out_shape=jax.ShapeDtypeStruct((rows, D_MODEL), F32),
        compiler_params=pltpu.CompilerParams(
            dimension_semantics=("arbitrary",), vmem_limit_bytes=V7X_VMEM_LIMIT),
        name="outproj",
    )(mixed_m, mixed_a, x, w_out, g_post)


def _rope_tables(pos):
    half = A_DK // 2
    inv = ROPE_THETA ** (-jnp.arange(half, dtype=F32) * 2.0 / A_DK)
    ang = pos.astype(F32)[:, None] * inv[None, :]
    cos, sin = jnp.cos(ang), jnp.sin(ang)
    reps = A_DV // A_DK
    return (jnp.tile(jnp.concatenate([cos, cos], axis=1), (1, reps)),
            jnp.tile(jnp.concatenate([-sin, sin], axis=1), (1, reps)))


def _row_tile(rows, want):
    return math.gcd(rows, want)


def _layer(x, pos, state, attend, weights, l, *, batch, seq, chunk, slab_dtype, mix_dtype):
    (w_main, wg, bias_row, bias_col, w_out, g_pre, g_post, g_mlstm) = weights
    rows = batch * seq
    x2 = x.reshape(rows, D_MODEL)
    cos, sin = _rope_tables(pos)
    slab, k_rot, v_new, fslab, gates, gatest = _inproj(
        x2, g_pre, w_main, wg, cos, sin, tm=_row_tile(rows, 512), slab_dtype=slab_dtype)
    nc = seq // chunk
    gatest3 = gatest.reshape(8, batch * nc, chunk).transpose(1, 0, 2)
    mixed_m, c_new, n_new, m_new = _mlstm(
        slab, fslab, gates, gatest3, bias_row, bias_col, g_mlstm, state,
        batch=batch, seq=seq, L=chunk, out_dtype=mix_dtype)
    mixed_a = attend(slab, k_rot, v_new, fslab)
    y = _outproj(mixed_m, mixed_a, x2, w_out, g_post, tm=_row_tile(rows, 512))
    return (y.reshape(batch, seq, D_MODEL),
            k_rot.reshape(batch, seq, A_H, 2 * A_DK), v_new.reshape(batch, seq, A_H, A_DV),
            c_new, n_new.reshape(batch, M_H, M_D), m_new.reshape(batch, M_H))


def kernel(x_prompt, x_sample, cache_k, cache_v, state_C, state_n, state_m, page_table, w_in, b_i, b_f, w_out, g_pre, g_post, g_mlstm, g_diff, lam_q1, lam_k1, lam_q2, lam_k2):
    depth = w_in.shape[0]
    B, T, _ = x_prompt.shape
    DB, TS, _ = x_sample.shape
    n_past = page_table.shape[1] * PAGE
    pos_p = jnp.arange(T)
    pos_s = jnp.tile(n_past + jnp.arange(TS), DB)
    assert B == 1

    xp, xs = x_prompt, x_sample
    outs = [[] for _ in range(10)]
    for l in range(depth):
        lam_init = 0.8 - 0.6 * math.exp(-0.3 * l)
        wt = w_in[l].T
        w_main = _repack_w_in(wt)
        wg = _gate_cols(wt)
        bias8 = jnp.concatenate([b_i[l], b_f[l]])
        bias_row = jnp.pad(bias8, (0, GATE_PAD - N_GATES)).reshape(1, GATE_PAD)
        bias_col = bias8.reshape(N_GATES, 1)
        weights = (w_main, wg, bias_row, bias_col, w_out[l].astype(BF16),
                   g_pre[l].reshape(1, D_MODEL), g_post[l].reshape(1, D_MODEL), g_mlstm[l].reshape(1, M_W))
        gd = g_diff[l].reshape(1, A_W)
        lam_params = jnp.stack([lam_q1[l], lam_k1[l], lam_q2[l], lam_k2[l]])

        def prompt_attend(slab, k_rot, v_new, fslab):
            return _prompt_attn(slab, fslab, gd, lam_params, seq=T, tq=_row_tile(T, 512), lam_init=lam_init)

        def sample_attend(slab, k_rot, v_new, fslab):
            return _sample_attn(page_table, slab, k_rot, v_new, fslab, cache_k[l], cache_v[l], gd, lam_params,
                                batch=DB, ts=TS, lam_init=lam_init)

        xp, kk, vv, C, n, m = _layer(xp, pos_p, None, prompt_attend, weights, l,
                                     batch=B, seq=T, chunk=math.gcd(T, 256), slab_dtype=BF16, mix_dtype=BF16)
        for lst, a in zip(outs[:5], (kk, vv, C, n, m)):
            lst.append(a)
        state = (state_C[l], state_n[l].reshape(DB, M_H, 1, M_D), state_m[l].reshape(DB, M_H, 1, 1))
        xs, kk, vv, C, n, m = _layer(xs, pos_s, state, sample_attend, weights, l,
                                     batch=DB, seq=TS, chunk=TS, slab_dtype=F32, mix_dtype=F32)
        for lst, a in zip(outs[5:], (kk, vv, C, n, m)):
            lst.append(a)

    return (xp, xs) + tuple(jnp.stack(o) for o in outs)
```

```python
import functools
import math

import jax
import jax.numpy as jnp
from jax import lax
from jax.experimental import pallas as pl
from jax.experimental.pallas import tpu as pltpu

F32 = jnp.float32
BF16 = jnp.bfloat16

D_MODEL = 2048
M_H = 4
M_D = 256
M_W = M_H * M_D
A_H = 8
A_DV = 128
A_DK = 64
A_W = A_H * A_DV
PAGE = 128
ROPE_THETA = 10000.0
EPS = 1e-6
PIECE = 1024
LANES = 128
MXU_N = 256
N_GATES = 2 * M_H
REPACK_W = 1024
GATE_PAD = LANES
HEADS_PER_STEP = 4
MLSTM_SEQS_PER_STEP = 4
SAMPLE_SEQS_PER_STEP = 2
SAMPLE_PAGES_PER_STEP = 16
ONES_ROWS = 16
V7X_VMEM_LIMIT = 56 * 1024 * 1024
Q_SCALE = A_DK ** -0.5 * math.log2(math.e)

_P_MQ, _P_MK, _P_MV, _P_AQ, _P_AK, _P_AV, _P_MO, _P_MZ, _P_AZ = range(9)
N_PIECES = 9
N_SLAB = 6
N_FSLAB = 3


def _dot(a, b):
    return jnp.dot(a, b, preferred_element_type=F32)


def _dot_nt(a, b):
    return lax.dot_general(a, b, (((1,), (1,)), ((), ())), preferred_element_type=F32)


def _dot_tn(a, b):
    return lax.dot_general(a, b, (((0,), (0,)), ((), ())), preferred_element_type=F32)


def _split2(a):
    hi = a.astype(BF16)
    lo = (a - hi.astype(F32)).astype(BF16)
    return hi, lo


def _split3(a):
    a1 = a.astype(BF16)
    r1 = a - a1.astype(F32)
    a2 = r1.astype(BF16)
    a3 = (r1 - a2.astype(F32)).astype(BF16)
    return a1, a2, a3


def _sigmoid(x):
    return 1.0 / (1.0 + jnp.exp(-x))


def _log_sigmoid(x):
    return jnp.minimum(x, 0.0) - jnp.log(1.0 + jnp.exp(-jnp.abs(x)))


def _repack_kernel(row_off, wt_ref, out_ref):
    del row_off
    out_ref[...] = wt_ref[...].T.astype(BF16)


def _repack_w_in(wt):
    gate_row = 5 * M_W
    a_row = gate_row + N_GATES
    src = {_P_MQ: 0, _P_MK: M_W, _P_MV: 2 * M_W, _P_MO: 3 * M_W, _P_MZ: 4 * M_W,
           _P_AQ: a_row, _P_AK: a_row + A_W, _P_AV: a_row + 2 * A_W, _P_AZ: a_row + 3 * A_W}
    offs = [src[piece] + k * REPACK_W for piece in range(N_PIECES) for k in range(PIECE // REPACK_W)]
    assert all(o % 8 == 0 for o in offs)
    grid_spec = pltpu.PrefetchScalarGridSpec(
        num_scalar_prefetch=1,
        grid=(len(offs),),
        in_specs=[pl.BlockSpec((pl.Element(REPACK_W), pl.Element(D_MODEL)),
                               lambda j, off: (pl.multiple_of(off[j], 8), 0))],
        out_specs=pl.BlockSpec((D_MODEL, REPACK_W), lambda j, off: (0, j)),
    )
    return pl.pallas_call(
        _repack_kernel,
        grid_spec=grid_spec,
        out_shape=jax.ShapeDtypeStruct((D_MODEL, N_PIECES * PIECE), BF16),
        compiler_params=pltpu.CompilerParams(dimension_semantics=("arbitrary",), vmem_limit_bytes=V7X_VMEM_LIMIT),
        name="repack_w_in",
    )(jnp.asarray(offs, jnp.int32), wt)


def _gate_cols_kernel(wt_ref, out_ref):
    lane = lax.broadcasted_iota(jnp.int32, out_ref.shape, 1)
    out_ref[...] = jnp.where(lane < N_GATES, wt_ref[...].T, 0.0)


def _gate_cols(wt):
    return pl.pallas_call(
        _gate_cols_kernel,
        grid=(1,),
        in_specs=[pl.BlockSpec((pl.Element(GATE_PAD), pl.Element(D_MODEL)), lambda i: (5 * M_W, 0))],
        out_specs=pl.BlockSpec((D_MODEL, GATE_PAD), lambda i: (0, 0)),
        out_shape=jax.ShapeDtypeStruct((D_MODEL, GATE_PAD), F32),
        name="gate_cols",
    )(wt)


def _inproj_kernel(x_ref, gpre_ref, w_ref, wg_ref, cos_ref, sin_ref,
                   slab_ref, k_ref, v_ref, fslab_ref, gates_ref, gatest_ref, h_ref):
    j = pl.program_id(1)
    tm = x_ref.shape[0]

    @pl.when(j == 0)
    def _():
        x = x_ref[...]
        h = x * lax.rsqrt(jnp.mean(x * x, axis=-1, keepdims=True) + EPS) * gpre_ref[...]
        h_hi, h_lo = _split2(h)
        h_ref[...] = h_hi
        wg_hi, wg_lo = _split2(wg_ref[...])
        hi_pass = _dot(h_hi, jnp.concatenate([wg_hi, wg_lo], axis=1))
        gates = _dot(h_lo, wg_hi) + hi_pass[:, GATE_PAD:] + hi_pass[:, :GATE_PAD]
        gates_ref[...] = gates
        gatest_ref[...] = gates.T[0:8, :]

    def matmul_chunks():
        for c in range(PIECE // MXU_N):
            cols = slice(c * MXU_N, (c + 1) * MXU_N)
            yield cols, _dot(h_ref[...], w_ref[:, cols])

    def rope_chunks():
        cos = cos_ref[...]
        sin = sin_ref[...]
        lane = lax.broadcasted_iota(jnp.int32, (tm, A_DV), 1)
        first_half = (lane % A_DK) < (A_DK // 2)
        for cols, acc in matmul_chunks():
            for c in range(MXU_N // A_DV):
                xc = acc[:, c * A_DV:(c + 1) * A_DV]
                rot = jnp.where(first_half, pltpu.roll(xc, A_DV - A_DK // 2, 1), pltpu.roll(xc, A_DK // 2, 1))
                yield slice(cols.start + c * A_DV, cols.start + (c + 1) * A_DV), xc * cos + rot * sin

    @pl.when((j == _P_MQ) | (j == _P_MV))
    def _():
        for cols, acc in matmul_chunks():
            slab_ref[:, cols] = acc.astype(slab_ref.dtype)

    @pl.when(j == _P_MK)
    def _():
        for cols, acc in matmul_chunks():
            slab_ref[:, cols] = (acc * (M_D ** -0.5)).astype(slab_ref.dtype)

    @pl.when(j == _P_AQ)
    def _():
        for cols, y in rope_chunks():
            slab_ref[:, cols] = (y * Q_SCALE).astype(slab_ref.dtype)

    @pl.when(j == _P_AK)
    def _():
        for cols, y in rope_chunks():
            k_ref[:, cols] = y
            slab_ref[:, cols] = y.astype(slab_ref.dtype)

    @pl.when(j == _P_AV)
    def _():
        for cols, acc in matmul_chunks():
            v_ref[:, cols] = acc
            slab_ref[:, cols] = acc.astype(slab_ref.dtype)

    @pl.when(j >= N_SLAB)
    def _():
        for cols, acc in matmul_chunks():
            fslab_ref[:, cols] = acc


def _inproj(x, g_pre, w_main, wg, cos, sin, *, tm, slab_dtype):
    rows = x.shape[0]
    assert rows % tm == 0
    grid = (rows // tm, N_PIECES)
    row_blk = lambda width: pl.BlockSpec((tm, width), lambda i, j: (i, 0))
    return pl.pallas_call(
        _inproj_kernel,
        grid=grid,
        in_specs=[
            row_blk(D_MODEL),
            pl.BlockSpec((1, D_MODEL), lambda i, j: (0, 0)),
            pl.BlockSpec((D_MODEL, PIECE), lambda i, j: (0, j)),
            pl.BlockSpec((D_MODEL, GATE_PAD), lambda i, j: (0, 0)),
            row_blk(A_DV),
            row_blk(A_DV),
        ],
        out_specs=[
            pl.BlockSpec((tm, PIECE), lambda i, j: (i, jnp.minimum(j, N_SLAB - 1))),
            row_blk(PIECE),
            row_blk(PIECE),
            pl.BlockSpec((tm, PIECE), lambda i, j: (i, jnp.maximum(j - N_SLAB, 0))),
            row_blk(GATE_PAD),
            pl.BlockSpec((8, tm), lambda i, j: (0, i)),
        ],
        out_shape=[
            jax.ShapeDtypeStruct((rows, N_SLAB * PIECE), slab_dtype),
            jax.ShapeDtypeStruct((rows, PIECE), F32),
            jax.ShapeDtypeStruct((rows, PIECE), F32),
            jax.ShapeDtypeStruct((rows, N_FSLAB * PIECE), F32),
            jax.ShapeDtypeStruct((rows, GATE_PAD), F32),
            jax.ShapeDtypeStruct((8, rows), F32),
        ],
        scratch_shapes=[pltpu.VMEM((tm, D_MODEL), BF16)],
        compiler_params=pltpu.CompilerParams(
            dimension_semantics=("arbitrary", "arbitrary"), vmem_limit_bytes=V7X_VMEM_LIMIT),
        name="inproj",
    )(x, g_pre, w_main, wg, cos, sin)


def _mlstm_kernel(*refs, L, seqs, has_state):
    if has_state:
        (q_ref, k_ref, v_ref, mo_ref, mz_ref, g_ref, gt_ref, bias_ref, biast_ref, gm_ref,
         c0_ref, n0_ref, m0_ref, out_ref, cout_ref, nout_ref, mout_ref, c_s, n_s, m_s) = refs
    else:
        (q_ref, k_ref, v_ref, mo_ref, mz_ref, g_ref, gt_ref, bias_ref, biast_ref, gm_ref,
         out_ref, cout_ref, nout_ref, mout_ref, c_s, n_s, m_s) = refs
    c = pl.program_id(1)

    @pl.when(c == 0)
    def _():
        if has_state:
            c_s[...] = c0_ref[...]
            n_s[...] = n0_ref[...]
            m_s[...] = m0_ref[...]
        else:
            c_s[...] = jnp.zeros_like(c_s)
            n_s[...] = jnp.zeros_like(n_s)
            m_s[...] = jnp.zeros_like(m_s)

    r_i = lax.broadcasted_iota(jnp.int32, (L, L), 0)
    c_i = lax.broadcasted_iota(jnp.int32, (L, L), 1)
    causal = r_i >= c_i
    gate_terms = []
    for bb in range(seqs):
        rows = slice(bb * L, (bb + 1) * L)
        gcol = g_ref[rows, :] + bias_ref[...]
        grow = gt_ref[bb] + biast_ref[...]
        lf_col = _log_sigmoid(gcol)
        lf_row = _log_sigmoid(grow)
        if L >= 128:
            tri = jnp.where(causal, 1.0, 0.0).astype(BF16)
            trit = jnp.where(r_i <= c_i, 1.0, 0.0).astype(BF16)
            a1, a2, a3 = _split3(lf_col)
            bcols = _dot(tri, a3) + _dot(tri, a2) + _dot(tri, a1)
            a1, a2, a3 = _split3(lf_row)
            brows = _dot(a3, trit) + _dot(a2, trit) + _dot(a1, trit)
        else:
            sums = [lf_col[0:1, :]]
            for t in range(1, L):
                sums.append(sums[-1] + lf_col[t:t + 1, :])
            bcols = jnp.concatenate(sums, axis=0)
            lane_t = lax.broadcasted_iota(jnp.int32, (8, L), 1)
            brows = jnp.zeros((8, L), F32)
            for s in range(L):
                brows = brows + jnp.where(lane_t >= s, lf_row[:, s:s + 1], 0.0)
        gate_terms.append((rows, gcol, grow, bcols, brows))
    for bb, hd in [(bb, hd) for bb in range(seqs) for hd in range(M_H)]:
        rows, gcol, grow, bcols, brows = gate_terms[bb]
        cols = slice(hd * M_D, (hd + 1) * M_D)
        q = q_ref[rows, cols].astype(BF16)
        k = k_ref[rows, cols].astype(BF16)
        v = v_ref[rows, cols].astype(BF16)
        i_col = gcol[:, hd:hd + 1]
        b_col = bcols[:, M_H + hd:M_H + hd + 1]
        i_row = grow[hd:hd + 1, :]
        b_row = brows[M_H + hd:M_H + hd + 1, :]

        m_prev = m_s[bb, hd]
        logd = jnp.where(causal, b_col - b_row + i_row, -jnp.inf)
        m_col = jnp.maximum(b_col + m_prev, jnp.max(logd, axis=1, keepdims=True))
        dmat = jnp.exp(logd - m_col)
        inter = jnp.exp(b_col + m_prev - m_col)

        cmat = c_s[bb, hd]
        nvec = n_s[bb, hd]
        w = _dot_nt(q, k) * dmat
        num = _dot(w.astype(BF16), v) + inter * _dot_nt(q, cmat.astype(BF16))
        qf = q.astype(F32)
        den = jnp.sum(w, axis=1, keepdims=True) + inter * jnp.sum(qf * nvec, axis=1, keepdims=True)
        hstate = num / jnp.maximum(jnp.abs(den), jnp.exp(-m_col))

        m_new = m_col[L - 1:L, :]
        b_last = b_col[L - 1:L, :]
        gdec = jnp.exp(b_last - b_col + i_col - m_new)
        decay = jnp.exp(b_last + m_prev - m_new)
        gk = gdec * k.astype(F32)
        c_s[bb, hd] = decay * cmat + _dot_tn(v, gk.astype(BF16))
        n_s[bb, hd] = decay * nvec + jnp.sum(gk, axis=0, keepdims=True)
        m_s[bb, hd] = m_new

        hm = _sigmoid(mo_ref[rows, cols]) * hstate
        hm = hm * lax.rsqrt(jnp.mean(hm * hm, axis=-1, keepdims=True) + EPS) * gm_ref[:, cols]
        mz = mz_ref[rows, cols]
        out_ref[rows, cols] = (hm * (mz * _sigmoid(mz))).astype(out_ref.dtype)

    @pl.when(c == pl.num_programs(1) - 1)
    def _():
        cout_ref[...] = c_s[...]
        nout_ref[...] = n_s[...]
        mout_ref[...] = m_s[...]


def _mlstm(slab, fslab, gates, gatest3, bias_row, bias_col, g_mlstm, state, *, batch, seq, L, out_dtype):
    nc = seq // L
    assert nc * L == seq
    has_state = state is not None
    seqs = math.gcd(batch, MLSTM_SEQS_PER_STEP) if nc == 1 else 1
    tok = lambda piece: pl.BlockSpec((seqs * L, M_W), lambda b, c: (b * nc + c, piece))
    const = lambda shape: pl.BlockSpec(shape, lambda b, c: (0,) * len(shape))
    per_b = lambda r, cdim: pl.BlockSpec((seqs, M_H, r, cdim), lambda b, c: (b, 0, 0, 0))
    in_specs = [
        tok(_P_MQ), tok(_P_MK), tok(_P_MV),
        tok(_P_MO - N_SLAB), tok(_P_MZ - N_SLAB),
        pl.BlockSpec((seqs * L, GATE_PAD), lambda b, c: (b * nc + c, 0)),
        pl.BlockSpec((seqs, 8, L), lambda b, c: (b * nc + c, 0, 0)),
        const((1, GATE_PAD)), const((8, 1)), const((1, M_W)),
    ]
    args = [slab, slab, slab, fslab, fslab, gates, gatest3, bias_row, bias_col, g_mlstm]
    if has_state:
        in_specs += [per_b(M_D, M_D), per_b(1, M_D), per_b(1, 1)]
        args += list(state)
    return pl.pallas_call(
        functools.partial(_mlstm_kernel, L=L, seqs=seqs, has_state=has_state),
        grid=(batch // seqs, nc),
        in_specs=in_specs,
        out_specs=[
            pl.BlockSpec((seqs * L, M_W), lambda b, c: (b * nc + c, 0)),
            per_b(M_D, M_D), per_b(1, M_D), per_b(1, 1),
        ],
        out_shape=[
            jax.ShapeDtypeStruct((batch * seq, M_W), out_dtype),
            jax.ShapeDtypeStruct((batch, M_H, M_D, M_D), F32),
            jax.ShapeDtypeStruct((batch, M_H, 1, M_D), F32),
            jax.ShapeDtypeStruct((batch, M_H, 1, 1), F32),
        ],
        scratch_shapes=[pltpu.VMEM((seqs, M_H, M_D, M_D), F32), pltpu.VMEM((seqs, M_H, 1, M_D), F32),
                        pltpu.VMEM((seqs, M_H, 1, 1), F32)],
        compiler_params=pltpu.CompilerParams(
            dimension_semantics=("arbitrary", "arbitrary"), vmem_limit_bytes=V7X_VMEM_LIMIT),
        name="mlstm_state" if has_state else "mlstm_fresh",
    )(*args)


def _lambda(lam_ref, lam_init):
    lam = lam_ref[...]
    s1 = jnp.sum(lam[0:1, :] * lam[1:2, :], axis=1, keepdims=True)
    s2 = jnp.sum(lam[2:3, :] * lam[3:4, :], axis=1, keepdims=True)
    return jnp.exp(s1) - jnp.exp(s2) + lam_init


def _stack_maps(qh):
    lane = lax.broadcasted_iota(jnp.int32, qh.shape, 1)
    zero = jnp.zeros_like(qh)
    return jnp.concatenate([jnp.where(lane < A_DK, qh, zero), jnp.where(lane >= A_DK, qh, zero)], axis=0)


def _online_softmax_step(s, pv_fn, m_s, l_s, acc_s):
    reps = s.shape[1] // A_DV
    m_prev = m_s[...]
    m_next = jnp.maximum(m_prev, jnp.max(s, axis=1, keepdims=True))
    alpha = jnp.exp2(m_prev - m_next)
    p = jnp.exp2(s - (jnp.tile(m_next, (1, reps)) if reps > 1 else m_next))
    l_s[...] = alpha * l_s[...] + jnp.sum(p, axis=1, keepdims=True)
    acc_s[...] = alpha * acc_s[...] + pv_fn(p.astype(BF16))
    m_s[...] = m_next


def _diff_finish(o, n, lam, gd, az, lam_init):
    ha = o[:n, :] - lam * o[n:, :]
    ha = ha * lax.rsqrt(jnp.mean(ha * ha, axis=-1, keepdims=True) + EPS) * gd * (1.0 - lam_init)
    return ha * (az * _sigmoid(az))


def _prompt_attn_kernel(q_ref, k_ref, v_ref, az_ref, gd_ref, lam_ref, out_ref, vt_s, sa_s, sb_s, m_s, acc_s, *, tq, lam_init):
    qi = pl.program_id(1)
    n_tiles = k_ref.shape[0] // tq
    heads = range(HEADS_PER_STEP)
    hcols = lambda hh: slice(hh * A_DV, (hh + 1) * A_DV)

    @pl.when(qi == 0)
    def _():
        def xpose(c, carry):
            start = pl.multiple_of(c * tq, tq)
            for hh in heads:
                vt = v_ref[pl.ds(start, tq), hcols(hh)].astype(F32).T.astype(BF16)
                vt_s[hh, c] = jnp.concatenate([vt, jnp.ones((ONES_ROWS, tq), BF16)], axis=0)
            return carry
        lax.fori_loop(0, n_tiles, xpose, 0)

    qs = [_stack_maps(q_ref[:, hcols(hh)]) for hh in heads]
    m_s[...] = jnp.full_like(m_s, -jnp.inf)
    acc_s[...] = jnp.zeros_like(acc_s)

    def scores(ki, dst):
        start = pl.multiple_of(ki * tq, tq)
        for hh in heads:
            dst[hh] = _dot_nt(k_ref[pl.ds(start, tq), hcols(hh)], qs[hh])

    def consume(src, ki, masked):
        for hh in heads:
            st = src[hh]
            if masked:
                key = lax.broadcasted_iota(jnp.int32, (tq, 2 * tq), 0)
                qry = lax.broadcasted_iota(jnp.int32, (tq, 2 * tq), 1)
                qry = jnp.where(qry >= tq, qry - tq, qry)
                st = jnp.where(qry >= key, st, -jnp.inf)
            m_prev = m_s[hh]
            m_next = jnp.maximum(m_prev, jnp.max(st, axis=0, keepdims=True))
            alpha = jnp.exp2(m_prev - m_next)
            pt = jnp.exp2(st - m_next).astype(BF16)
            acc_s[hh] = alpha * acc_s[hh] + _dot(vt_s[hh, ki], pt)
            m_s[hh] = m_next

    scores(0, sa_s)

    def pair(j, carry):
        scores(2 * j + 1, sb_s)
        consume(sa_s, 2 * j, False)
        scores(2 * j + 2, sa_s)
        consume(sb_s, 2 * j + 1, False)
        return carry

    lax.fori_loop(0, qi // 2, pair, 0)

    @pl.when(qi % 2 == 0)
    def _():
        consume(sa_s, qi, True)

    @pl.when(qi % 2 == 1)
    def _():
        scores(qi, sb_s)
        consume(sa_s, qi - 1, False)
        consume(sb_s, qi, True)

    lam = _lambda(lam_ref, lam_init)
    for hh in heads:
        acc = acc_s[hh]
        ot = acc[:A_DV, :] / acc[A_DV:A_DV + 1, :]
        hat = ot[:, :tq] - lam * ot[:, tq:]
        hat = hat * lax.rsqrt(jnp.mean(hat * hat, axis=0, keepdims=True) + EPS)
        az = az_ref[:, hcols(hh)]
        out_ref[:, hcols(hh)] = (hat.T * (gd_ref[:, hcols(hh)] * (1.0 - lam_init))
                                 * (az * _sigmoid(az))).astype(out_ref.dtype)


def _prompt_attn(slab, fslab, g_diff, lam_params, *, seq, tq, lam_init):
    nq = seq // tq
    assert nq * tq == seq
    hp = HEADS_PER_STEP
    width = hp * A_DV
    blk = PIECE // width
    return pl.pallas_call(
        functools.partial(_prompt_attn_kernel, tq=tq, lam_init=lam_init),
        grid=(A_H // hp, nq),
        in_specs=[
            pl.BlockSpec((tq, width), lambda h, i: (i, _P_AQ * blk + h)),
            pl.BlockSpec((seq, width), lambda h, i: (0, _P_AK * blk + h), pipeline_mode=pl.Buffered(1)),
            pl.BlockSpec((seq, width), lambda h, i: (0, _P_AV * blk + h), pipeline_mode=pl.Buffered(1)),
            pl.BlockSpec((tq, width), lambda h, i: (i, (_P_AZ - N_SLAB) * blk + h)),
            pl.BlockSpec((1, width), lambda h, i: (0, h)),
            pl.BlockSpec((4, A_DK), lambda h, i: (0, 0)),
        ],
        out_specs=pl.BlockSpec((tq, width), lambda h, i: (i, h)),
        out_shape=jax.ShapeDtypeStruct((seq, A_W), BF16),
        scratch_shapes=[pltpu.VMEM((hp, nq, A_DV + ONES_ROWS, tq), BF16),
                        pltpu.VMEM((hp, tq, 2 * tq), F32), pltpu.VMEM((hp, tq, 2 * tq), F32),
                        pltpu.VMEM((hp, 1, 2 * tq), F32), pltpu.VMEM((hp, A_DV + ONES_ROWS, 2 * tq), F32)],
        compiler_params=pltpu.CompilerParams(
            dimension_semantics=("arbitrary", "arbitrary"), vmem_limit_bytes=V7X_VMEM_LIMIT),
        name="prompt_attn",
    )(slab, slab, slab, fslab, g_diff, lam_params)


def _sample_attn_kernel(pt_ref, q_ref, kn_ref, vn_ref, az_ref, *rest, ts, seqs, group, lam_init):
    del pt_ref
    n_pg = seqs * group
    ck_refs, cv_refs = rest[:n_pg], rest[n_pg:2 * n_pg]
    gd_ref, lam_ref, out_ref, qs_s, m_s, l_s, acc_s = rest[2 * n_pg:]
    p = pl.program_id(1)
    n2 = 2 * ts

    @pl.when(p == 0)
    def _():
        zero = jnp.zeros((n2, A_DV), F32)
        for sq in range(seqs):
            q = q_ref[sq * ts:(sq + 1) * ts, :]
            qbd = jnp.concatenate(
                [jnp.concatenate([_stack_maps(q[:, h * A_DV:(h + 1) * A_DV]) if hc == h else zero
                                  for hc in range(A_H)], axis=1) for h in range(A_H)], axis=0)
            qs_s[sq] = qbd.T.astype(BF16)
        m_s[...] = jnp.full_like(m_s, -jnp.inf)
        l_s[...] = jnp.zeros_like(l_s)
        acc_s[...] = jnp.zeros_like(acc_s)

    def attend(sq, k_tiles, v_tiles, mask=None):
        st = sum(_dot(jnp.concatenate(k_tiles[h], axis=0), qs_s[sq, h * A_DV:(h + 1) * A_DV, :])
                 for h in range(A_H))
        s = st.T
        if mask is not None:
            s = jnp.where(mask, s, -jnp.inf)

        def pv(pb):
            return jnp.concatenate(
                [sum(_dot(pb[h * n2:(h + 1) * n2, g * PAGE:(g + 1) * PAGE], vt) for g, vt in enumerate(v_tiles[h]))
                 for h in range(A_H)], axis=0)

        _online_softmax_step(s, pv, m_s.at[sq], l_s.at[sq], acc_s.at[sq])

    head_rows = lambda ref, h: ref[pl.ds(h, PAGE, stride=A_H), :].astype(BF16)
    for sq in range(seqs):
        mine = slice(sq * group, (sq + 1) * group)
        attend(sq, [[head_rows(r, h) for r in ck_refs[mine]] for h in range(A_H)],
               [[head_rows(r, h) for r in cv_refs[mine]] for h in range(A_H)])

    @pl.when(p == pl.num_programs(1) - 1)
    def _():
        lam = _lambda(lam_ref, lam_init)
        gd = gd_ref[...]
        pad = jnp.zeros((PAGE - ts, A_DV), F32)
        head_new = lambda a, h: jnp.concatenate([a[:, h * A_DV:(h + 1) * A_DV], pad], axis=0).astype(BF16)
        r = lax.broadcasted_iota(jnp.int32, (A_H * n2, PAGE), 0)
        cc = lax.broadcasted_iota(jnp.int32, (A_H * n2, PAGE), 1)
        for sq in range(seqs):
            rows = slice(sq * ts, (sq + 1) * ts)
            kn = kn_ref[rows, :]
            vn = vn_ref[rows, :]
            az = az_ref[rows, :]
            attend(sq, [[head_new(kn, h)] for h in range(A_H)], [[head_new(vn, h)] for h in range(A_H)],
                   mask=(r % ts) >= cc)
            o = acc_s[sq] / l_s[sq]
            for h in range(A_H):
                cols = slice(h * A_DV, (h + 1) * A_DV)
                out_ref[rows, cols] = _diff_finish(o[h * n2:(h + 1) * n2, :], ts, lam, gd[:, cols], az[:, cols], lam_init)


def _sample_attn(page_table, slab, k_new, v_new, fslab, cache_k, cache_v, g_diff, lam_params, *, batch, ts, lam_init):
    n_pages = page_table.shape[1]
    pool = cache_k.shape[0]
    seqs = math.gcd(batch, SAMPLE_SEQS_PER_STEP)
    group = math.gcd(n_pages, SAMPLE_PAGES_PER_STEP // seqs)
    ck = cache_k.reshape(pool, PAGE * A_H, A_DV)
    cv = cache_v.reshape(pool, PAGE * A_H, A_DV)
    rows = A_H * 2 * ts
    tok = lambda col: pl.BlockSpec((seqs * ts, PIECE), lambda b, p, pt: (b, col))
    page = lambda sq, g: pl.BlockSpec((None, PAGE * A_H, A_DV),
                                      lambda b, p, pt: (pt[b * seqs + sq, p * group + g], 0, 0))
    pages = [page(sq, g) for sq in range(seqs) for g in range(group)]
    grid_spec = pltpu.PrefetchScalarGridSpec(
        num_scalar_prefetch=1,
        grid=(batch // seqs, n_pages // group),
        in_specs=[tok(_P_AQ), tok(0), tok(0), tok(_P_AZ - N_SLAB)] + pages + pages + [
            pl.BlockSpec((1, A_W), lambda b, p, pt: (0, 0)),
            pl.BlockSpec((4, A_DK), lambda b, p, pt: (0, 0)),
        ],
        out_specs=pl.BlockSpec((seqs * ts, A_W), lambda b, p, pt: (b, 0)),
        scratch_shapes=[pltpu.VMEM((seqs, A_W, rows), BF16)] + [pltpu.VMEM((seqs, rows, A_DV), F32)] * 3,
    )
    n_pg = seqs * group
    return pl.pallas_call(
        functools.partial(_sample_attn_kernel, ts=ts, seqs=seqs, group=group, lam_init=lam_init),
        grid_spec=grid_spec,
        out_shape=jax.ShapeDtypeStruct((batch * ts, A_W), F32),
        compiler_params=pltpu.CompilerParams(
            dimension_semantics=("arbitrary", "arbitrary"), vmem_limit_bytes=V7X_VMEM_LIMIT),
        name="sample_attn",
    )(page_table, slab, k_new, v_new, fslab, *([ck] * n_pg), *([cv] * n_pg), g_diff, lam_params)


def _outproj_kernel(mm_ref, ma_ref, x_ref, w_ref, g_ref, out_ref):
    y = (_dot(mm_ref[...].astype(BF16), w_ref[0:M_W, :])
         + _dot(ma_ref[...].astype(BF16), w_ref[M_W:M_W + A_W, :]))
    out_ref[...] = x_ref[...] + y * lax.rsqrt(jnp.mean(y * y, axis=-1, keepdims=True) + EPS) * g_ref[...]


def _outproj(mixed_m, mixed_a, x, w_out, g_post, *, tm):
    rows = x.shape[0]
    assert rows % tm == 0
    return pl.pallas_call(
        _outproj_kernel,
        grid=(rows // tm,),
        in_specs=[
            pl.BlockSpec((tm, M_W), lambda i: (i, 0)),
            pl.BlockSpec((tm, A_W), lambda i: (i, 0)),
            pl.BlockSpec((tm, D_MODEL), lambda i: (i, 0)),
            pl.BlockSpec((M_W + A_W, D_MODEL), lambda i: (0, 0)),
            pl.BlockSpec((1, D_MODEL), lambda i: (0, 0)),
        ],
        out_specs=pl.BlockSpec((tm, D_MODEL), lambda i: (i, 0)),
        out_shape=jax.ShapeDtypeStruct((rows, D_MODEL), F32),
        compiler_params=pltpu.CompilerParams(
            dimension_semantics=("arbitrary",), vmem_limit_bytes=V7X_VMEM_LIMIT),
        name="outproj",
    )(mixed_m, mixed_a, x, w_out, g_post)


def _rope_tables(pos):
    half = A_DK // 2
    inv = ROPE_THETA ** (-jnp.arange(half, dtype=F32) * 2.0 / A_DK)
    ang = pos.astype(F32)[:, None] * inv[None, :]
    cos, sin = jnp.cos(ang), jnp.sin(ang)
    reps = A_DV // A_DK
    return (jnp.tile(jnp.concatenate([cos, cos], axis=1), (1, reps)),
            jnp.tile(jnp.concatenate([-sin, sin], axis=1), (1, reps)))


def _row_tile(rows, want):
    return math.gcd(rows, want)


def _layer(x, pos, state, attend, weights, l, *, batch, seq, chunk, slab_dtype, mix_dtype):
    (w_main, wg, bias_row, bias_col, w_out, g_pre, g_post, g_mlstm) = weights
    rows = batch * seq
    x2 = x.reshape(rows, D_MODEL)
    cos, sin = _rope_tables(pos)
    slab, k_rot, v_new, fslab, gates, gatest = _inproj(
        x2, g_pre, w_main, wg, cos, sin, tm=_row_tile(rows, 512), slab_dtype=slab_dtype)
    nc = seq // chunk
    gatest3 = gatest.reshape(8, batch * nc, chunk).transpose(1, 0, 2)
    mixed_m, c_new, n_new, m_new = _mlstm(
        slab, fslab, gates, gatest3, bias_row, bias_col, g_mlstm, state,
        batch=batch, seq=seq, L=chunk, out_dtype=mix_dtype)
    mixed_a = attend(slab, k_rot, v_new, fslab)
    y = _outproj(mixed_m, mixed_a, x2, w_out, g_post, tm=_row_tile(rows, 512))
    return (y.reshape(batch, seq, D_MODEL),
            k_rot.reshape(batch, seq, A_H, 2 * A_DK), v_new.reshape(batch, seq, A_H, A_DV),
            c_new, n_new.reshape(batch, M_H, M_D), m_new.reshape(batch, M_H))


def kernel(x_prompt, x_sample, cache_k, cache_v, state_C, state_n, state_m, page_table, w_in, b_i, b_f, w_out, g_pre, g_post, g_mlstm, g_diff, lam_q1, lam_k1, lam_q2, lam_k2):
    depth = w_in.shape[0]
    B, T, _ = x_prompt.shape
    DB, TS, _ = x_sample.shape
    n_past = page_table.shape[1] * PAGE
    pos_p = jnp.arange(T)
    pos_s = jnp.tile(n_past + jnp.arange(TS), DB)
    assert B == 1

    xp, xs = x_prompt, x_sample
    outs = [[] for _ in range(10)]
    for l in range(depth):
        lam_init = 0.8 - 0.6 * math.exp(-0.3 * l)
        wt = w_in[l].T
        w_main = _repack_w_in(wt)
        wg = _gate_cols(wt)
        bias8 = jnp.concatenate([b_i[l], b_f[l]])
        bias_row = jnp.pad(bias8, (0, GATE_PAD - N_GATES)).reshape(1, GATE_PAD)
        bias_col = bias8.reshape(N_GATES, 1)
        weights = (w_main, wg, bias_row, bias_col, w_out[l].astype(BF16),
                   g_pre[l].reshape(1, D_MODEL), g_post[l].reshape(1, D_MODEL), g_mlstm[l].reshape(1, M_W))
        gd = g_diff[l].reshape(1, A_W)
        lam_params = jnp.stack([lam_q1[l], lam_k1[l], lam_q2[l], lam_k2[l]])

        def prompt_attend(slab, k_rot, v_new, fslab):
            return _prompt_attn(slab, fslab, gd, lam_params, seq=T, tq=_row_tile(T, 512), lam_init=lam_init)

        def sample_attend(slab, k_rot, v_new, fslab):
            return _sample_attn(page_table, slab, k_rot, v_new, fslab, cache_k[l], cache_v[l], gd, lam_params,
                                batch=DB, ts=TS, lam_init=lam_init)

        xp, kk, vv, C, n, m = _layer(xp, pos_p, None, prompt_attend, weights, l,
                                     batch=B, seq=T, chunk=math.gcd(T, 256), slab_dtype=BF16, mix_dtype=BF16)
        for lst, a in zip(outs[:5], (kk, vv, C, n, m)):
            lst.append(a)
        state = (state_C[l], state_n[l].reshape(DB, M_H, 1, M_D), state_m[l].reshape(DB, M_H, 1, 1))
        xs, kk, vv, C, n, m = _layer(xs, pos_s, state, sample_attend, weights, l,
                                     batch=DB, seq=TS, chunk=TS, slab_dtype=F32, mix_dtype=F32)
        for lst, a in zip(outs[5:], (kk, vv, C, n, m)):
            lst.append(a)

    return (xp, xs) + tuple(jnp.stack(o) for o in outs)
```

```python
import functools
import math

import jax
import jax.numpy as jnp
from jax import lax
from jax.experimental import pallas as pl
from jax.experimental.pallas import tpu as pltpu

F32 = jnp.float32
BF16 = jnp.bfloat16

D_MODEL = 2048
M_H = 4
M_D = 256
M_W = M_H * M_D
A_H = 8
A_DV = 128
A_DK = 64
A_W = A_H * A_DV
PAGE = 128
ROPE_THETA = 10000.0
EPS = 1e-6
PIECE = 1024
LANES = 128
MXU_N = 256
N_GATES = 2 * M_H
REPACK_W = 1024
GATE_PAD = LANES
HEADS_PER_STEP = 4
MLSTM_SEQS_PER_STEP = 4
SAMPLE_SEQS_PER_STEP = 2
SAMPLE_PAGES_PER_STEP = 16
ONES_ROWS = 16
V7X_VMEM_LIMIT = 56 * 1024 * 1024
Q_SCALE = A_DK ** -0.5 * math.log2(math.e)

_P_MQ, _P_MK, _P_MV, _P_AQ, _P_AK, _P_AV, _P_MO, _P_MZ, _P_AZ = range(9)
N_PIECES = 9
N_SLAB = 6
N_FSLAB = 3


def _dot(a, b):
    return jnp.dot(a, b, preferred_element_type=F32)


def _dot_nt(a, b):
    return lax.dot_general(a, b, (((1,), (1,)), ((), ())), preferred_element_type=F32)


def _dot_tn(a, b):
    return lax.dot_general(a, b, (((0,), (0,)), ((), ())), preferred_element_type=F32)


def _split2(a):
    hi = a.astype(BF16)
    lo = (a - hi.astype(F32)).astype(BF16)
    return hi, lo


def _split3(a):
    a1 = a.astype(BF16)
    r1 = a - a1.astype(F32)
    a2 = r1.astype(BF16)
    a3 = (r1 - a2.astype(F32)).astype(BF16)
    return a1, a2, a3


def _sigmoid(x):
    return 1.0 / (1.0 + jnp.exp(-x))


def _log_sigmoid(x):
    return jnp.minimum(x, 0.0) - jnp.log(1.0 + jnp.exp(-jnp.abs(x)))


def _repack_kernel(row_off, wt_ref, out_ref):
    del row_off
    out_ref[...] = wt_ref[...].T.astype(BF16)


def _repack_w_in(wt):
    gate_row = 5 * M_W
    a_row = gate_row + N_GATES
    src = {_P_MQ: 0, _P_MK: M_W, _P_MV: 2 * M_W, _P_MO: 3 * M_W, _P_MZ: 4 * M_W,
           _P_AQ: a_row, _P_AK: a_row + A_W, _P_AV: a_row + 2 * A_W, _P_AZ: a_row + 3 * A_W}
    offs = [src[piece] + k * REPACK_W for piece in range(N_PIECES) for k in range(PIECE // REPACK_W)]
    assert all(o % 8 == 0 for o in offs) and REPACK_W == PIECE
    grid_spec = pltpu.PrefetchScalarGridSpec(
        num_scalar_prefetch=1,
        grid=(len(offs),),
        in_specs=[pl.BlockSpec((pl.Element(REPACK_W), pl.Element(D_MODEL)),
                               lambda j, off: (pl.multiple_of(off[j], 8), 0))],
        out_specs=pl.BlockSpec((None, D_MODEL, REPACK_W), lambda j, off: (j, 0, 0)),
    )
    return pl.pallas_call(
        _repack_kernel,
        grid_spec=grid_spec,
        out_shape=jax.ShapeDtypeStruct((N_PIECES, D_MODEL, PIECE), BF16),
        compiler_params=pltpu.CompilerParams(dimension_semantics=("arbitrary",), vmem_limit_bytes=V7X_VMEM_LIMIT),
        name="repack_w_in",
    )(jnp.asarray(offs, jnp.int32), wt)


def _gate_cols_kernel(wt_ref, out_ref):
    lane = lax.broadcasted_iota(jnp.int32, out_ref.shape, 1)
    out_ref[...] = jnp.where(lane < N_GATES, wt_ref[...].T, 0.0)


def _gate_cols(wt):
    return pl.pallas_call(
        _gate_cols_kernel,
        grid=(1,),
        in_specs=[pl.BlockSpec((pl.Element(GATE_PAD), pl.Element(D_MODEL)), lambda i: (5 * M_W, 0))],
        out_specs=pl.BlockSpec((D_MODEL, GATE_PAD), lambda i: (0, 0)),
        out_shape=jax.ShapeDtypeStruct((D_MODEL, GATE_PAD), F32),
        name="gate_cols",
    )(wt)


def _inproj_kernel(x_ref, gpre_ref, w_ref, wg_ref, cos_ref, sin_ref,
                   slab_ref, k_ref, v_ref, fslab_ref, gates_ref, gatest_ref, h_ref):
    j = pl.program_id(1)
    tm = x_ref.shape[0]

    @pl.when(j == 0)
    def _():
        x = x_ref[...]
        h = x * lax.rsqrt(jnp.mean(x * x, axis=-1, keepdims=True) + EPS) * gpre_ref[...]
        h_hi, h_lo = _split2(h)
        h_ref[...] = h_hi
        wg_hi, wg_lo = _split2(wg_ref[...])
        hi_pass = _dot(h_hi, jnp.concatenate([wg_hi, wg_lo], axis=1))
        gates = _dot(h_lo, wg_hi) + hi_pass[:, GATE_PAD:] + hi_pass[:, :GATE_PAD]
        gates_ref[...] = gates
        gatest_ref[...] = gates.T[0:8, :]

    def matmul_chunks():
        for c in range(PIECE // MXU_N):
            cols = slice(c * MXU_N, (c + 1) * MXU_N)
            yield cols, _dot(h_ref[...], w_ref[:, cols])

    def rope_chunks():
        cos = cos_ref[...]
        sin = sin_ref[...]
        lane = lax.broadcasted_iota(jnp.int32, (tm, A_DV), 1)
        first_half = (lane % A_DK) < (A_DK // 2)
        for cols, acc in matmul_chunks():
            for c in range(MXU_N // A_DV):
                xc = acc[:, c * A_DV:(c + 1) * A_DV]
                rot = jnp.where(first_half, pltpu.roll(xc, A_DV - A_DK // 2, 1), pltpu.roll(xc, A_DK // 2, 1))
                yield slice(cols.start + c * A_DV, cols.start + (c + 1) * A_DV), xc * cos + rot * sin

    @pl.when((j == _P_MQ) | (j == _P_MV))
    def _():
        for cols, acc in matmul_chunks():
            slab_ref[:, cols] = acc.astype(slab_ref.dtype)

    @pl.when(j == _P_MK)
    def _():
        for cols, acc in matmul_chunks():
            slab_ref[:, cols] = (acc * (M_D ** -0.5)).astype(slab_ref.dtype)

    @pl.when(j == _P_AQ)
    def _():
        for cols, y in rope_chunks():
            slab_ref[:, cols] = (y * Q_SCALE).astype(slab_ref.dtype)

    @pl.when(j == _P_AK)
    def _():
        for cols, y in rope_chunks():
            k_ref[:, cols] = y
            slab_ref[:, cols] = y.astype(slab_ref.dtype)

    @pl.when(j == _P_AV)
    def _():
        for cols, acc in matmul_chunks():
            v_ref[:, cols] = acc
            slab_ref[:, cols] = acc.astype(slab_ref.dtype)

    @pl.when(j >= N_SLAB)
    def _():
        for cols, acc in matmul_chunks():
            fslab_ref[:, cols] = acc


def _inproj(x, g_pre, w_main, wg, cos, sin, *, tm, slab_dtype):
    rows = x.shape[0]
    assert rows % tm == 0
    grid = (rows // tm, N_PIECES)
    row_blk = lambda width: pl.BlockSpec((tm, width), lambda i, j: (i, 0))
    return pl.pallas_call(
        _inproj_kernel,
        grid=grid,
        in_specs=[
            row_blk(D_MODEL),
            pl.BlockSpec((1, D_MODEL), lambda i, j: (0, 0)),
            pl.BlockSpec((None, D_MODEL, PIECE), lambda i, j: (j, 0, 0)),
            pl.BlockSpec((D_MODEL, GATE_PAD), lambda i, j: (0, 0)),
            row_blk(A_DV),
            row_blk(A_DV),
        ],
        out_specs=[
            pl.BlockSpec((tm, PIECE), lambda i, j: (i, jnp.minimum(j, N_SLAB - 1))),
            row_blk(PIECE),
            row_blk(PIECE),
            pl.BlockSpec((tm, PIECE), lambda i, j: (i, jnp.maximum(j - N_SLAB, 0))),
            row_blk(GATE_PAD),
            pl.BlockSpec((8, tm), lambda i, j: (0, i)),
        ],
        out_shape=[
            jax.ShapeDtypeStruct((rows, N_SLAB * PIECE), slab_dtype),
            jax.ShapeDtypeStruct((rows, PIECE), F32),
            jax.ShapeDtypeStruct((rows, PIECE), F32),
            jax.ShapeDtypeStruct((rows, N_FSLAB * PIECE), F32),
            jax.ShapeDtypeStruct((rows, GATE_PAD), F32),
            jax.ShapeDtypeStruct((8, rows), F32),
        ],
        scratch_shapes=[pltpu.VMEM((tm, D_MODEL), BF16)],
        compiler_params=pltpu.CompilerParams(
            dimension_semantics=("arbitrary", "arbitrary"), vmem_limit_bytes=V7X_VMEM_LIMIT),
        name="inproj",
    )(x, g_pre, w_main, wg, cos, sin)


def _mlstm_kernel(*refs, L, seqs, has_state):
    if has_state:
        (q_ref, k_ref, v_ref, mo_ref, mz_ref, g_ref, gt_ref, bias_ref, biast_ref, gm_ref,
         c0_ref, n0_ref, m0_ref, out_ref, cout_ref, nout_ref, mout_ref, c_s, n_s, m_s) = refs
    else:
        (q_ref, k_ref, v_ref, mo_ref, mz_ref, g_ref, gt_ref, bias_ref, biast_ref, gm_ref,
         out_ref, cout_ref, nout_ref, mout_ref, c_s, n_s, m_s) = refs
    c = pl.program_id(1)

    @pl.when(c == 0)
    def _():
        if has_state:
            c_s[...] = c0_ref[...]
            n_s[...] = n0_ref[...]
            m_s[...] = m0_ref[...]
        else:
            c_s[...] = jnp.zeros_like(c_s)
            n_s[...] = jnp.zeros_like(n_s)
            m_s[...] = jnp.zeros_like(m_s)

    r_i = lax.broadcasted_iota(jnp.int32, (L, L), 0)
    c_i = lax.broadcasted_iota(jnp.int32, (L, L), 1)
    causal = r_i >= c_i
    gate_terms = []
    for bb in range(seqs):
        rows = slice(bb * L, (bb + 1) * L)
        gcol = g_ref[rows, :] + bias_ref[...]
        grow = gt_ref[bb] + biast_ref[...]
        lf_col = _log_sigmoid(gcol)
        lf_row = _log_sigmoid(grow)
        if L >= 128:
            tri = jnp.where(causal, 1.0, 0.0).astype(BF16)
            trit = jnp.where(r_i <= c_i, 1.0, 0.0).astype(BF16)
            a1, a2, a3 = _split3(lf_col)
            bcols = _dot(tri, a3) + _dot(tri, a2) + _dot(tri, a1)
            a1, a2, a3 = _split3(lf_row)
            brows = _dot(a3, trit) + _dot(a2, trit) + _dot(a1, trit)
        else:
            sums = [lf_col[0:1, :]]
            for t in range(1, L):
                sums.append(sums[-1] + lf_col[t:t + 1, :])
            bcols = jnp.concatenate(sums, axis=0)
            lane_t = lax.broadcasted_iota(jnp.int32, (8, L), 1)
            brows = jnp.zeros((8, L), F32)
            for s in range(L):
                brows = brows + jnp.where(lane_t >= s, lf_row[:, s:s + 1], 0.0)
        gate_terms.append((rows, gcol, grow, bcols, brows))
    for bb, hd in [(bb, hd) for bb in range(seqs) for hd in range(M_H)]:
        rows, gcol, grow, bcols, brows = gate_terms[bb]
        cols = slice(hd * M_D, (hd + 1) * M_D)
        q = q_ref[rows, cols].astype(BF16)
        k = k_ref[rows, cols].astype(BF16)
        v = v_ref[rows, cols].astype(BF16)
        i_col = gcol[:, hd:hd + 1]
        b_col = bcols[:, M_H + hd:M_H + hd + 1]
        i_row = grow[hd:hd + 1, :]
        b_row = brows[M_H + hd:M_H + hd + 1, :]

        m_prev = m_s[bb, hd]
        logd = jnp.where(causal, b_col - b_row + i_row, -jnp.inf)
        m_col = jnp.maximum(b_col + m_prev, jnp.max(logd, axis=1, keepdims=True))
        dmat = jnp.exp(logd - m_col)
        inter = jnp.exp(b_col + m_prev - m_col)

        cmat = c_s[bb, hd]
        nvec = n_s[bb, hd]
        w = _dot_nt(q, k) * dmat
        num = _dot(w.astype(BF16), v) + inter * _dot_nt(q, cmat.astype(BF16))
        qf = q.astype(F32)
        den = jnp.sum(w, axis=1, keepdims=True) + inter * jnp.sum(qf * nvec, axis=1, keepdims=True)
        hstate = num / jnp.maximum(jnp.abs(den), jnp.exp(-m_col))

        m_new = m_col[L - 1:L, :]
        b_last = b_col[L - 1:L, :]
        gdec = jnp.exp(b_last - b_col + i_col - m_new)
        decay = jnp.exp(b_last + m_prev - m_new)
        gk = gdec * k.astype(F32)
        c_s[bb, hd] = decay * cmat + _dot_tn(v, gk.astype(BF16))
        n_s[bb, hd] = decay * nvec + jnp.sum(gk, axis=0, keepdims=True)
        m_s[bb, hd] = m_new

        hm = _sigmoid(mo_ref[rows, cols]) * hstate
        hm = hm * lax.rsqrt(jnp.mean(hm * hm, axis=-1, keepdims=True) + EPS) * gm_ref[:, cols]
        mz = mz_ref[rows, cols]
        out_ref[rows, cols] = (hm * (mz * _sigmoid(mz))).astype(out_ref.dtype)

    @pl.when(c == pl.num_programs(1) - 1)
    def _():
        cout_ref[...] = c_s[...]
        nout_ref[...] = n_s[...]
        mout_ref[...] = m_s[...]


def _mlstm(slab, fslab, gates, gatest3, bias_row, bias_col, g_mlstm, state, *, batch, seq, L, out_dtype):
    nc = seq // L
    assert nc * L == seq
    has_state = state is not None
    seqs = math.gcd(batch, MLSTM_SEQS_PER_STEP) if nc == 1 else 1
    tok = lambda piece: pl.BlockSpec((seqs * L, M_W), lambda b, c: (b * nc + c, piece))
    const = lambda shape: pl.BlockSpec(shape, lambda b, c: (0,) * len(shape))
    per_b = lambda r, cdim: pl.BlockSpec((seqs, M_H, r, cdim), lambda b, c: (b, 0, 0, 0))
    in_specs = [
        tok(_P_MQ), tok(_P_MK), tok(_P_MV),
        tok(_P_MO - N_SLAB), tok(_P_MZ - N_SLAB),
        pl.BlockSpec((seqs * L, GATE_PAD), lambda b, c: (b * nc + c, 0)),
        pl.BlockSpec((seqs, 8, L), lambda b, c: (b * nc + c, 0, 0)),
        const((1, GATE_PAD)), const((8, 1)), const((1, M_W)),
    ]
    args = [slab, slab, slab, fslab, fslab, gates, gatest3, bias_row, bias_col, g_mlstm]
    if has_state:
        in_specs += [per_b(M_D, M_D), per_b(1, M_D), per_b(1, 1)]
        args += list(state)
    return pl.pallas_call(
        functools.partial(_mlstm_kernel, L=L, seqs=seqs, has_state=has_state),
        grid=(batch // seqs, nc),
        in_specs=in_specs,
        out_specs=[
            pl.BlockSpec((seqs * L, M_W), lambda b, c: (b * nc + c, 0)),
            per_b(M_D, M_D), per_b(1, M_D), per_b(1, 1),
        ],
        out_shape=[
            jax.ShapeDtypeStruct((batch * seq, M_W), out_dtype),
            jax.ShapeDtypeStruct((batch, M_H, M_D, M_D), F32),
            jax.ShapeDtypeStruct((batch, M_H, 1, M_D), F32),
            jax.ShapeDtypeStruct((batch, M_H, 1, 1), F32),
        ],
        scratch_shapes=[pltpu.VMEM((seqs, M_H, M_D, M_D), F32), pltpu.VMEM((seqs, M_H, 1, M_D), F32),
                        pltpu.VMEM((seqs, M_H, 1, 1), F32)],
        compiler_params=pltpu.CompilerParams(
            dimension_semantics=("arbitrary", "arbitrary"), vmem_limit_bytes=V7X_VMEM_LIMIT),
        name="mlstm_state" if has_state else "mlstm_fresh",
    )(*args)


def _lambda(lam_ref, lam_init):
    lam = lam_ref[...]
    s1 = jnp.sum(lam[0:1, :] * lam[1:2, :], axis=1, keepdims=True)
    s2 = jnp.sum(lam[2:3, :] * lam[3:4, :], axis=1, keepdims=True)
    return jnp.exp(s1) - jnp.exp(s2) + lam_init


def _stack_maps(qh):
    lane = lax.broadcasted_iota(jnp.int32, qh.shape, 1)
    zero = jnp.zeros_like(qh)
    return jnp.concatenate([jnp.where(lane < A_DK, qh, zero), jnp.where(lane >= A_DK, qh, zero)], axis=0)


def _online_softmax_step(s, pv_fn, m_s, l_s, acc_s):
    reps = s.shape[1] // A_DV
    m_prev = m_s[...]
    m_next = jnp.maximum(m_prev, jnp.max(s, axis=1, keepdims=True))
    alpha = jnp.exp2(m_prev - m_next)
    p = jnp.exp2(s - (jnp.tile(m_next, (1, reps)) if reps > 1 else m_next))
    l_s[...] = alpha * l_s[...] + jnp.sum(p, axis=1, keepdims=True)
    acc_s[...] = alpha * acc_s[...] + pv_fn(p.astype(BF16))
    m_s[...] = m_next


def _diff_finish(o, n, lam, gd, az, lam_init):
    ha = o[:n, :] - lam * o[n:, :]
    ha = ha * lax.rsqrt(jnp.mean(ha * ha, axis=-1, keepdims=True) + EPS) * gd * (1.0 - lam_init)
    return ha * (az * _sigmoid(az))


def _prompt_attn_kernel(q_ref, k_ref, v_ref, az_ref, gd_ref, lam_ref, out_ref, vt_s, sa_s, sb_s, m_s, acc_s, *, tq, lam_init):
    qi = pl.program_id(1)
    n_tiles = k_ref.shape[0] // tq
    heads = range(HEADS_PER_STEP)
    hcols = lambda hh: slice(hh * A_DV, (hh + 1) * A_DV)

    @pl.when(qi == 0)
    def _():
        def xpose(c, carry):
            start = pl.multiple_of(c * tq, tq)
            for hh in heads:
                vt = v_ref[pl.ds(start, tq), hcols(hh)].astype(F32).T.astype(BF16)
                vt_s[hh, c] = jnp.concatenate([vt, jnp.ones((ONES_ROWS, tq), BF16)], axis=0)
            return carry
        lax.fori_loop(0, n_tiles, xpose, 0)

    qs = [_stack_maps(q_ref[:, hcols(hh)]) for hh in heads]
    m_s[...] = jnp.full_like(m_s, -jnp.inf)
    acc_s[...] = jnp.zeros_like(acc_s)

    def scores(ki, dst):
        start = pl.multiple_of(ki * tq, tq)
        for hh in heads:
            dst[hh] = _dot_nt(k_ref[pl.ds(start, tq), hcols(hh)], qs[hh])

    def consume(src, ki, masked):
        for hh in heads:
            st = src[hh]
            if masked:
                key = lax.broadcasted_iota(jnp.int32, (tq, 2 * tq), 0)
                qry = lax.broadcasted_iota(jnp.int32, (tq, 2 * tq), 1)
                qry = jnp.where(qry >= tq, qry - tq, qry)
                st = jnp.where(qry >= key, st, -jnp.inf)
            m_prev = m_s[hh]
            m_next = jnp.maximum(m_prev, jnp.max(st, axis=0, keepdims=True))
            alpha = jnp.exp2(m_prev - m_next)
            pt = jnp.exp2(st - m_next).astype(BF16)
            acc_s[hh] = alpha * acc_s[hh] + _dot(vt_s[hh, ki], pt)
            m_s[hh] = m_next

    scores(0, sa_s)

    def pair(j, carry):
        scores(2 * j + 1, sb_s)
        consume(sa_s, 2 * j, False)
        scores(2 * j + 2, sa_s)
        consume(sb_s, 2 * j + 1, False)
        return carry

    lax.fori_loop(0, qi // 2, pair, 0)

    @pl.when(qi % 2 == 0)
    def _():
        consume(sa_s, qi, True)

    @pl.when(qi % 2 == 1)
    def _():
        scores(qi, sb_s)
        consume(sa_s, qi - 1, False)
        consume(sb_s, qi, True)

    lam = _lambda(lam_ref, lam_init)
    for hh in heads:
        acc = acc_s[hh]
        ot = acc[:A_DV, :] / acc[A_DV:A_DV + 1, :]
        hat = ot[:, :tq] - lam * ot[:, tq:]
        hat = hat * lax.rsqrt(jnp.mean(hat * hat, axis=0, keepdims=True) + EPS)
        az = az_ref[:, hcols(hh)]
        out_ref[:, hcols(hh)] = (hat.T * (gd_ref[:, hcols(hh)] * (1.0 - lam_init))
                                 * (az * _sigmoid(az))).astype(out_ref.dtype)


def _prompt_attn(slab, fslab, g_diff, lam_params, *, seq, tq, lam_init):
    nq = seq // tq
    assert nq * tq == seq
    hp = HEADS_PER_STEP
    width = hp * A_DV
    blk = PIECE // width
    return pl.pallas_call(
        functools.partial(_prompt_attn_kernel, tq=tq, lam_init=lam_init),
        grid=(A_H // hp, nq),
        in_specs=[
            pl.BlockSpec((tq, width), lambda h, i: (i, _P_AQ * blk + h)),
            pl.BlockSpec((seq, width), lambda h, i: (0, _P_AK * blk + h), pipeline_mode=pl.Buffered(1)),
            pl.BlockSpec((seq, width), lambda h, i: (0, _P_AV * blk + h), pipeline_mode=pl.Buffered(1)),
            pl.BlockSpec((tq, width), lambda h, i: (i, (_P_AZ - N_SLAB) * blk + h)),
            pl.BlockSpec((1, width), lambda h, i: (0, h)),
            pl.BlockSpec((4, A_DK), lambda h, i: (0, 0)),
        ],
        out_specs=pl.BlockSpec((tq, width), lambda h, i: (i, h)),
        out_shape=jax.ShapeDtypeStruct((seq, A_W), BF16),
        scratch_shapes=[pltpu.VMEM((hp, nq, A_DV + ONES_ROWS, tq), BF16),
                        pltpu.VMEM((hp, tq, 2 * tq), F32), pltpu.VMEM((hp, tq, 2 * tq), F32),
                        pltpu.VMEM((hp, 1, 2 * tq), F32), pltpu.VMEM((hp, A_DV + ONES_ROWS, 2 * tq), F32)],
        compiler_params=pltpu.CompilerParams(
            dimension_semantics=("arbitrary", "arbitrary"), vmem_limit_bytes=V7X_VMEM_LIMIT),
        name="prompt_attn",
    )(slab, slab, slab, fslab, g_diff, lam_params)


def _sample_attn_kernel(pt_ref, q_ref, kn_ref, vn_ref, az_ref, *rest, ts, seqs, group, lam_init):
    del pt_ref
    n_pg = seqs * group
    ck_refs, cv_refs = rest[:n_pg], rest[n_pg:2 * n_pg]
    gd_ref, lam_ref, out_ref, qs_s, m_s, l_s, acc_s = rest[2 * n_pg:]
    p = pl.program_id(1)
    n2 = 2 * ts

    @pl.when(p == 0)
    def _():
        zero = jnp.zeros((n2, A_DV), F32)
        for sq in range(seqs):
            q = q_ref[sq * ts:(sq + 1) * ts, :]
            qbd = jnp.concatenate(
                [jnp.concatenate([_stack_maps(q[:, h * A_DV:(h + 1) * A_DV]) if hc == h else zero
                                  for hc in range(A_H)], axis=1) for h in range(A_H)], axis=0)
            qs_s[sq] = qbd.T.astype(BF16)
        m_s[...] = jnp.full_like(m_s, -jnp.inf)
        l_s[...] = jnp.zeros_like(l_s)
        acc_s[...] = jnp.zeros_like(acc_s)

    def attend(sq, k_tiles, v_tiles, mask=None):
        st = sum(_dot(jnp.concatenate(k_tiles[h], axis=0), qs_s[sq, h * A_DV:(h + 1) * A_DV, :])
                 for h in range(A_H))
        s = st.T
        if mask is not None:
            s = jnp.where(mask, s, -jnp.inf)

        def pv(pb):
            return jnp.concatenate(
                [sum(_dot(pb[h * n2:(h + 1) * n2, g * PAGE:(g + 1) * PAGE], vt) for g, vt in enumerate(v_tiles[h]))
                 for h in range(A_H)], axis=0)

        _online_softmax_step(s, pv, m_s.at[sq], l_s.at[sq], acc_s.at[sq])

    head_rows = lambda ref, h: ref[pl.ds(h, PAGE, stride=A_H), :].astype(BF16)
    for sq in range(seqs):
        mine = slice(sq * group, (sq + 1) * group)
        attend(sq, [[head_rows(r, h) for r in ck_refs[mine]] for h in range(A_H)],
               [[head_rows(r, h) for r in cv_refs[mine]] for h in range(A_H)])

    @pl.when(p == pl.num_programs(1) - 1)
    def _():
        lam = _lambda(lam_ref, lam_init)
        gd = gd_ref[...]
        pad = jnp.zeros((PAGE - ts, A_DV), F32)
        head_new = lambda a, h: jnp.concatenate([a[:, h * A_DV:(h + 1) * A_DV], pad], axis=0).astype(BF16)
        r = lax.broadcasted_iota(jnp.int32, (A_H * n2, PAGE), 0)
        cc = lax.broadcasted_iota(jnp.int32, (A_H * n2, PAGE), 1)
        for sq in range(seqs):
            rows = slice(sq * ts, (sq + 1) * ts)
            kn = kn_ref[rows, :]
            vn = vn_ref[rows, :]
            az = az_ref[rows, :]
            attend(sq, [[head_new(kn, h)] for h in range(A_H)], [[head_new(vn, h)] for h in range(A_H)],
                   mask=(r % ts) >= cc)
            o = acc_s[sq] / l_s[sq]
            for h in range(A_H):
                cols = slice(h * A_DV, (h + 1) * A_DV)
                out_ref[rows, cols] = _diff_finish(o[h * n2:(h + 1) * n2, :], ts, lam, gd[:, cols], az[:, cols], lam_init)


def _sample_attn(page_table, slab, k_new, v_new, fslab, cache_k, cache_v, g_diff, lam_params, *, batch, ts, lam_init):
    n_pages = page_table.shape[1]
    pool = cache_k.shape[0]
    seqs = math.gcd(batch, SAMPLE_SEQS_PER_STEP)
    group = math.gcd(n_pages, SAMPLE_PAGES_PER_STEP // seqs)
    ck = cache_k.reshape(pool, PAGE * A_H, A_DV)
    cv = cache_v.reshape(pool, PAGE * A_H, A_DV)
    rows = A_H * 2 * ts
    tok = lambda col: pl.BlockSpec((seqs * ts, PIECE), lambda b, p, pt: (b, col))
    page = lambda sq, g: pl.BlockSpec((None, PAGE * A_H, A_DV),
                                      lambda b, p, pt: (pt[b * seqs + sq, p * group + g], 0, 0))
    pages = [page(sq, g) for sq in range(seqs) for g in range(group)]
    grid_spec = pltpu.PrefetchScalarGridSpec(
        num_scalar_prefetch=1,
        grid=(batch // seqs, n_pages // group),
        in_specs=[tok(_P_AQ), tok(0), tok(0), tok(_P_AZ - N_SLAB)] + pages + pages + [
            pl.BlockSpec((1, A_W), lambda b, p, pt: (0, 0)),
            pl.BlockSpec((4, A_DK), lambda b, p, pt: (0, 0)),
        ],
        out_specs=pl.BlockSpec((seqs * ts, A_W), lambda b, p, pt: (b, 0)),
        scratch_shapes=[pltpu.VMEM((seqs, A_W, rows), BF16)] + [pltpu.VMEM((seqs, rows, A_DV), F32)] * 3,
    )
    n_pg = seqs * group
    return pl.pallas_call(
        functools.partial(_sample_attn_kernel, ts=ts, seqs=seqs, group=group, lam_init=lam_init),
        grid_spec=grid_spec,
        out_shape=jax.ShapeDtypeStruct((batch * ts, A_W), F32),
        compiler_params=pltpu.CompilerParams(
            dimension_semantics=("arbitrary", "arbitrary"), vmem_limit_bytes=V7X_VMEM_LIMIT),
        name="sample_attn",
    )(page_table, slab, k_new, v_new, fslab, *([ck] * n_pg), *([cv] * n_pg), g_diff, lam_params)


def _outproj_kernel(mm_ref, ma_ref, x_ref, w_ref, g_ref, out_ref):
    y = (_dot(mm_ref[...].astype(BF16), w_ref[0:M_W, :])
         + _dot(ma_ref[...].astype(BF16), w_ref[M_W:M_W + A_W, :]))
    out_ref[...] = x_ref[...] + y * lax.rsqrt(jnp.mean(y * y, axis=-1, keepdims=True) + EPS) * g_ref[...]


def _outproj(mixed_m, mixed_a, x, w_out, g_post, *, tm):
    rows = x.shape[0]
    assert rows % tm == 0
    return pl.pallas_call(
        _outproj_kernel,
        grid=(rows // tm,),
        in_specs=[
            pl.BlockSpec((tm, M_W), lambda i: (i, 0)),
            pl.BlockSpec((tm, A_W), lambda i: (i, 0)),
            pl.BlockSpec((tm, D_MODEL), lambda i: (i, 0)),
            pl.BlockSpec((M_W + A_W, D_MODEL), lambda i: (0, 0)),
            pl.BlockSpec((1, D_MODEL), lambda i: (0, 0)),
        ],
        out_specs=pl.BlockSpec((tm, D_MODEL), lambda i: (i, 0)),
        out_shape=jax.ShapeDtypeStruct((rows, D_MODEL), F32),
        compiler_params=pltpu.CompilerParams(
            dimension_semantics=("arbitrary",), vmem_limit_bytes=V7X_VMEM_LIMIT),
        name="outproj",
    )(mixed_m, mixed_a, x, w_out, g_post)


def _rope_tables(pos):
    half = A_DK // 2
    inv = ROPE_THETA ** (-jnp.arange(half, dtype=F32) * 2.0 / A_DK)
    ang = pos.astype(F32)[:, None] * inv[None, :]
    cos, sin = jnp.cos(ang), jnp.sin(ang)
    reps = A_DV // A_DK
    return (jnp.tile(jnp.concatenate([cos, cos], axis=1), (1, reps)),
            jnp.tile(jnp.concatenate([-sin, sin], axis=1), (1, reps)))


def _row_tile(rows, want):
    return math.gcd(rows, want)


def _layer(x, pos, state, attend, weights, l, *, batch, seq, chunk, slab_dtype, mix_dtype):
    (w_main, wg, bias_row, bias_col, w_out, g_pre, g_post, g_mlstm) = weights
    rows = batch * seq
    x2 = x.reshape(rows, D_MODEL)
    cos, sin = _rope_tables(pos)
    slab, k_rot, v_new, fslab, gates, gatest = _inproj(
        x2, g_pre, w_main, wg, cos, sin, tm=_row_tile(rows, 512), slab_dtype=slab_dtype)
    nc = seq // chunk
    gatest3 = gatest.reshape(8, batch * nc, chunk).transpose(1, 0, 2)
    mixed_m, c_new, n_new, m_new = _mlstm(
        slab, fslab, gates, gatest3, bias_row, bias_col, g_mlstm, state,
        batch=batch, seq=seq, L=chunk, out_dtype=mix_dtype)
    mixed_a = attend(slab, k_rot, v_new, fslab)
    y = _outproj(mixed_m, mixed_a, x2, w_out, g_post, tm=_row_tile(rows, 512))
    return (y.reshape(batch, seq, D_MODEL),
            k_rot.reshape(batch, seq, A_H, 2 * A_DK), v_new.reshape(batch, seq, A_H, A_DV),
            c_new, n_new.reshape(batch, M_H, M_D), m_new.reshape(batch, M_H))


def kernel(x_prompt, x_sample, cache_k, cache_v, state_C, state_n, state_m, page_table, w_in, b_i, b_f, w_out, g_pre, g_post, g_mlstm, g_diff, lam_q1, lam_k1, lam_q2, lam_k2):
    depth = w_in.shape[0]
    B, T, _ = x_prompt.shape
    DB, TS, _ = x_sample.shape
    n_past = page_table.shape[1] * PAGE
    pos_p = jnp.arange(T)
    pos_s = jnp.tile(n_past + jnp.arange(TS), DB)
    assert B == 1

    xp, xs = x_prompt, x_sample
    outs = [[] for _ in range(10)]
    for l in range(depth):
        lam_init = 0.8 - 0.6 * math.exp(-0.3 * l)
        wt = w_in[l].T
        w_main = _repack_w_in(wt)
        wg = _gate_cols(wt)
        bias8 = jnp.concatenate([b_i[l], b_f[l]])
        bias_row = jnp.pad(bias8, (0, GATE_PAD - N_GATES)).reshape(1, GATE_PAD)
        bias_col = bias8.reshape(N_GATES, 1)
        weights = (w_main, wg, bias_row, bias_col, w_out[l].astype(BF16),
                   g_pre[l].reshape(1, D_MODEL), g_post[l].reshape(1, D_MODEL), g_mlstm[l].reshape(1, M_W))
        gd = g_diff[l].reshape(1, A_W)
        lam_params = jnp.stack([lam_q1[l], lam_k1[l], lam_q2[l], lam_k2[l]])

        def prompt_attend(slab, k_rot, v_new, fslab):
            return _prompt_attn(slab, fslab, gd, lam_params, seq=T, tq=_row_tile(T, 512), lam_init=lam_init)

        def sample_attend(slab, k_rot, v_new, fslab):
            return _sample_attn(page_table, slab, k_rot, v_new, fslab, cache_k[l], cache_v[l], gd, lam_params,
                                batch=DB, ts=TS, lam_init=lam_init)

        xp, kk, vv, C, n, m = _layer(xp, pos_p, None, prompt_attend, weights, l,
                                     batch=B, seq=T, chunk=math.gcd(T, 256), slab_dtype=BF16, mix_dtype=BF16)
        for lst, a in zip(outs[:5], (kk, vv, C, n, m)):
            lst.append(a)
        state = (state_C[l], state_n[l].reshape(DB, M_H, 1, M_D), state_m[l].reshape(DB, M_H, 1, 1))
        xs, kk, vv, C, n, m = _layer(xs, pos_s, state, sample_attend, weights, l,
                                     batch=DB, seq=TS, chunk=TS, slab_dtype=F32, mix_dtype=F32)
        for lst, a in zip(outs[5:], (kk, vv, C, n, m)):
            lst.append(a)

    return (xp, xs) + tuple(jnp.stack(o) for o in outs)
```
